```python
import jax, jax.numpy as jnp
from jax import lax
import numpy as np

D_MODEL = 1024
BATCH = 8
SEQ = 4096
DEPTH = 1

N_META = 16
N_HEADS = 8
HEAD_DIM = 128
D_ATTN = N_HEADS * HEAD_DIM
D_CONV = D_MODEL
CONV_WIDTH = 3
D_FF = 2816
Q_BLOCK = 128
N_BRANCH = 2
RMS_EPS = 1e-6

D_IN = 3 * D_ATTN + N_HEADS + 3 * D_CONV + N_BRANCH * D_MODEL
SPLIT_POINTS = [D_ATTN, 2 * D_ATTN, 3 * D_ATTN, 3 * D_ATTN + N_HEADS,
                3 * D_ATTN + N_HEADS + D_CONV, 3 * D_ATTN + N_HEADS + 2 * D_CONV,
                3 * D_ATTN + N_HEADS + 3 * D_CONV]

kernel_name = "hybrid_fox_shortconv_gated_merge"


def rms_norm(x, g):
    xf = x.astype(jnp.float32)
    y = xf * lax.rsqrt(jnp.mean(xf * xf, axis=-1, keepdims=True) + RMS_EPS)
    return (y * g.astype(jnp.float32)).astype(x.dtype)


def causal_dwconv(u, w):
    L = u.shape[1]
    up = jnp.pad(u, ((0, 0), (CONV_WIDTH - 1, 0), (0, 0)))
    out = up[:, 0:L] * w[0]
    for k in range(1, CONV_WIDTH):
        out = out + up[:, k:k + L] * w[k]
    return out


def forgetting_attention(q, k, v, cum_logf):
    B, L = q.shape[0], q.shape[1]
    n_real = L - N_META
    n_blk = n_real // Q_BLOCK
    scale = HEAD_DIM ** -0.5
    cum_t = jnp.transpose(cum_logf, (0, 2, 1))
    kpos = jnp.arange(L)

    def attend(args):
        q_blk, c_blk, qpos = args
        s = jnp.einsum('bqhd,bkhd->bhqk', q_blk, k,
                       preferred_element_type=jnp.float32) * scale
        s = s + (c_blk[..., :, None] - cum_t[..., None, :])
        mask = kpos[None, :] <= qpos[:, None]
        s = jnp.where(mask, s, -jnp.inf)
        p = jax.nn.softmax(s, axis=-1)
        return jnp.einsum('bhqk,bkhd->bqhd', p.astype(v.dtype), v)

    o_meta = attend((q[:, :N_META], cum_t[:, :, :N_META], jnp.arange(N_META)))
    q_r = q[:, N_META:].reshape(B, n_blk, Q_BLOCK, N_HEADS, HEAD_DIM).transpose(1, 0, 2, 3, 4)
    c_r = cum_t[:, :, N_META:].reshape(B, N_HEADS, n_blk, Q_BLOCK).transpose(2, 0, 1, 3)
    pos_r = (N_META + jnp.arange(n_real)).reshape(n_blk, Q_BLOCK)
    o_r = lax.map(attend, (q_r, c_r, pos_r))
    o_r = o_r.transpose(1, 0, 2, 3, 4).reshape(B, n_real, N_HEADS, HEAD_DIM)
    return jnp.concatenate([o_meta, o_r], axis=1)


def hybrid_mixer(h, w_in, b_f, conv_w, w_o_attn, w_o_conv, w_o):
    B, L, _ = h.shape
    proj = h @ w_in
    q, k, v, f_logit, gb, gc, u, gates = jnp.split(proj, SPLIT_POINTS, axis=-1)
    log_f = jax.nn.log_sigmoid((f_logit + b_f).astype(jnp.float32))
    cum_logf = jnp.cumsum(log_f, axis=1)
    att = forgetting_attention(q.reshape(B, L, N_HEADS, HEAD_DIM),
                               k.reshape(B, L, N_HEADS, HEAD_DIM),
                               v.reshape(B, L, N_HEADS, HEAD_DIM), cum_logf)
    y_att = att.reshape(B, L, D_ATTN) @ w_o_attn
    y_conv = (gb * causal_dwconv(gc * u, conv_w)) @ w_o_conv
    g_att, g_conv = jnp.split(jax.nn.sigmoid(gates), N_BRANCH, axis=-1)
    return (g_att * y_att + g_conv * y_conv) @ w_o


def conv_glu(h, w_ffn_in, ffn_conv_w, w_ffn_out):
    a, val = jnp.split(h @ w_ffn_in, 2, axis=-1)
    return (jax.nn.silu(causal_dwconv(a, ffn_conv_w)) * val) @ w_ffn_out


def setup_inputs(seed: int = 0) -> dict:
    key = jax.random.key(seed)
    ks = jax.random.split(key, 16)
    nrm = lambda k, shape, fan_in: jax.random.normal(k, shape, jnp.float32) * (fan_in ** -0.5)
    return {
        "x": jax.random.normal(ks[0], (BATCH, SEQ, D_MODEL), jnp.float32),
        "meta_tokens": jax.random.normal(ks[1], (N_META, D_MODEL), jnp.float32),
        "g_mix": 1.0 + 0.02 * jax.random.normal(ks[2], (DEPTH, D_MODEL), jnp.float32),
        "w_in": nrm(ks[3], (DEPTH, D_MODEL, D_IN), D_MODEL),
        "b_f": jax.random.uniform(ks[4], (DEPTH, N_HEADS), jnp.float32, 1.0, 6.0),
        "conv_w": nrm(ks[5], (DEPTH, CONV_WIDTH, D_CONV), CONV_WIDTH),
        "w_o_attn": nrm(ks[6], (DEPTH, D_ATTN, D_MODEL), D_ATTN),
        "w_o_conv": nrm(ks[7], (DEPTH, D_CONV, D_MODEL), D_CONV),
        "w_o": nrm(ks[8], (DEPTH, D_MODEL, D_MODEL), D_MODEL),
        "g_ffn": 1.0 + 0.02 * jax.random.normal(ks[9], (DEPTH, D_MODEL), jnp.float32),
        "w_ffn_in": nrm(ks[10], (DEPTH, D_MODEL, 2 * D_FF), D_MODEL),
        "ffn_conv_w": nrm(ks[11], (DEPTH, CONV_WIDTH, D_FF), CONV_WIDTH),
        "w_ffn_out": nrm(ks[12], (DEPTH, D_FF, D_MODEL), D_FF),
        "g_final": 1.0 + 0.02 * jax.random.normal(ks[13], (D_MODEL,), jnp.float32),
    }


def reference(x, meta_tokens, g_mix, w_in, b_f, conv_w, w_o_attn, w_o_conv, w_o,
              g_ffn, w_ffn_in, ffn_conv_w, w_ffn_out, g_final):
    B = x.shape[0]
    meta = jnp.broadcast_to(meta_tokens.astype(x.dtype)[None], (B, N_META, x.shape[-1]))
    z = jnp.concatenate([meta, x], axis=1)
    for l in range(DEPTH):
        z = z + hybrid_mixer(rms_norm(z, g_mix[l]), w_in[l], b_f[l], conv_w[l],
                             w_o_attn[l], w_o_conv[l], w_o[l])
        z = z + conv_glu(rms_norm(z, g_ffn[l]), w_ffn_in[l], ffn_conv_w[l], w_ffn_out[l])
    return rms_norm(z, g_final)[:, N_META:]
```

```python
import functools

import jax
import jax.numpy as jnp
from jax import lax
from jax.experimental import pallas as pl
from jax.experimental.pallas import tpu as pltpu

D_MODEL = 1024
N_HEADS = 8
HEAD_DIM = 128
N_META = 16
D_FF = 2816
CONV_WIDTH = 3
RMS_EPS = 1e-6

LANES = 128
SUBLANES = 8
HALO = SUBLANES
VMEM_LIMIT = 56 * 1024 * 1024

LOG2E = 1.4426950408889634
Q_SCALE = HEAD_DIM ** -0.5 * LOG2E
MASKED = -1e30

ROW_TILE_IN = 512
ROW_TILE_OUT = 256
Q_TILE = 512
FF_CHUNKS = ((0, 1536), (1536, D_FF))

BF16 = jnp.bfloat16
F32 = jnp.float32


def _dot(a, b):
    return jnp.dot(a, b, preferred_element_type=F32)


def _rms_norm(x, g):
    return x * lax.rsqrt(jnp.mean(x * x, axis=-1, keepdims=True) + RMS_EPS) * g


def _sigmoid(x):
    return 1.0 / (1.0 + jnp.exp(-x))


def _cumsum_rows(x):
    t = x.shape[0]
    row = lax.broadcasted_iota(jnp.int32, (t, t), 0)
    col = lax.broadcasted_iota(jnp.int32, (t, t), 1)
    tri = jnp.where(col <= row, 1.0, 0.0).astype(BF16)
    hi = x.astype(BF16)
    rest = x - hi.astype(F32)
    mid = rest.astype(BF16)
    lo = (rest - mid.astype(F32)).astype(BF16)
    return _dot(tri, hi) + _dot(tri, mid) + _dot(tri, lo)


def _causal_conv(scr, lo, hi, t, w_ref):
    out = scr[HALO - 2:HALO - 2 + t, lo:hi] * w_ref[0:1, lo:hi]
    out = out + scr[HALO - 1:HALO - 1 + t, lo:hi] * w_ref[1:2, lo:hi]
    return out + scr[HALO:HALO + t, lo:hi] * w_ref[2:3, lo:hi]


def _mixer_in_kernel(x_ref, g_ref, wqkv_ref, wf_ref, bf_ref, wcv_ref, cw_ref, halo_ref, c0_ref,
                     q_ref, k_ref, v_ref, c_ref, yc_ref, *rest, emit_tail):
    if emit_tail:
        tail_ref, cu_scr, carry = rest
    else:
        cu_scr, carry = rest
    t = x_ref.shape[0]

    @pl.when(pl.program_id(1) == 0)
    def _():
        cu_scr[0:HALO, :] = halo_ref[...]
        carry[...] = c0_ref[...]

    h = _rms_norm(x_ref[...], g_ref[...]).astype(BF16)

    q_ref[...] = (_dot(h, wqkv_ref[:, 0:D_MODEL]) * Q_SCALE).astype(BF16)
    k_ref[...] = _dot(h, wqkv_ref[:, D_MODEL:2 * D_MODEL]).astype(BF16)
    v_ref[...] = _dot(h, wqkv_ref[:, 2 * D_MODEL:3 * D_MODEL]).astype(BF16)

    f = _dot(h, wf_ref[...]) + bf_ref[...]
    log_f = (jnp.minimum(f, 0.0) - jnp.log1p(jnp.exp(-jnp.abs(f)))) * LOG2E
    c = _cumsum_rows(log_f) + carry[...]
    c_ref[...] = c
    carry[...] = c[t - 1:t, :]

    gb = _dot(h, wcv_ref[:, 0:D_MODEL])
    gc = _dot(h, wcv_ref[:, D_MODEL:2 * D_MODEL])
    u = _dot(h, wcv_ref[:, 2 * D_MODEL:3 * D_MODEL])
    cu_scr[HALO:HALO + t, :] = gc * u
    yc_ref[...] = (gb * _causal_conv(cu_scr, 0, D_MODEL, t, cw_ref)).astype(BF16)
    if emit_tail:
        tail_ref[...] = cu_scr[t:t + HALO, :]
    cu_scr[0:HALO, :] = cu_scr[t:t + HALO, :]


def _resident(shape):
    return pl.BlockSpec(shape, lambda *_: (0,) * len(shape), pipeline_mode=pl.Buffered(1))


def _mixer_in(x, g_mix, w_qkv, w_f, b_f, w_cv, conv_w, halo, c0, *, tile, emit_tail):
    b, n, _ = x.shape
    rows = pl.BlockSpec((None, tile, D_MODEL), lambda bi, i: (bi, i, 0))
    out_shape = [jax.ShapeDtypeStruct((b, n, D_MODEL), BF16)] * 3
    out_shape += [jax.ShapeDtypeStruct((b, n, LANES), F32), jax.ShapeDtypeStruct((b, n, D_MODEL), BF16)]
    out_specs = [rows, rows, rows, pl.BlockSpec((None, tile, LANES), lambda bi, i: (bi, i, 0)), rows]
    if emit_tail:
        assert b == 1 and n == tile
        out_shape.append(jax.ShapeDtypeStruct((HALO, D_MODEL), F32))
        out_specs.append(pl.BlockSpec((HALO, D_MODEL), lambda bi, i: (0, 0)))
    return pl.pallas_call(
        functools.partial(_mixer_in_kernel, emit_tail=emit_tail),
        grid=(b, n // tile),
        in_specs=[rows, _resident((1, D_MODEL)), _resident(w_qkv.shape), _resident(w_f.shape),
                  _resident(b_f.shape), _resident(w_cv.shape), _resident(conv_w.shape),
                  _resident(halo.shape), _resident(c0.shape)],
        out_specs=out_specs,
        out_shape=out_shape,
        scratch_shapes=[pltpu.VMEM((tile + HALO, D_MODEL), F32), pltpu.VMEM((1, LANES), F32)],
        compiler_params=pltpu.CompilerParams(dimension_semantics=("arbitrary", "arbitrary"),
                                             vmem_limit_bytes=VMEM_LIMIT),
        name="mixer_in_meta" if emit_tail else "mixer_in",
    )(x, g_mix, w_qkv, w_f, b_f, w_cv, conv_w, halo, c0)


def _attention_kernel(q_ref, k_ref, v_ref, ccol_ref, crow_ref, *rest, has_prefix):
    if has_prefix:
        kp_ref, vp_ref, cprow_ref, o_ref, m_scr, l_scr, acc_scr = rest
    else:
        o_ref, m_scr, l_scr, acc_scr = rest
    tq = q_ref.shape[0]
    head = pl.program_id(1)
    i = pl.program_id(2)

    q = q_ref[...]
    lane = lax.broadcasted_iota(jnp.int32, (tq, LANES), 1)
    ccol = jnp.sum(jnp.where(lane == head, ccol_ref[...], 0.0), axis=1, keepdims=True)

    m_scr[...] = jnp.full(m_scr.shape, MASKED, F32)
    l_scr[...] = jnp.zeros(l_scr.shape, F32)
    acc_scr[...] = jnp.zeros(acc_scr.shape, F32)

    def attend(kb, vb, crow, mask):
        s = lax.dot_general(q, kb, (((1,), (1,)), ((), ())), preferred_element_type=F32)
        s = s + (ccol - crow)
        if mask is not None:
            s = jnp.where(mask, s, MASKED)
        m_old = m_scr[...]
        m_new = jnp.maximum(m_old, jnp.max(s, axis=1, keepdims=True))
        alpha = jnp.exp2(m_old - m_new)
        p = jnp.exp2(s - m_new)
        l_scr[...] = alpha * l_scr[...] + jnp.sum(p, axis=1, keepdims=True)
        acc_scr[...] = alpha * acc_scr[...] + _dot(p.astype(BF16), vb)
        m_scr[...] = m_new

    if has_prefix:
        attend(kp_ref[...], vp_ref[...], cprow_ref[...], None)

    def attend_at(off, mask):
        attend(k_ref[pl.ds(off, tq), :], v_ref[pl.ds(off, tq), :], crow_ref[:, pl.ds(off, tq)], mask)

    row = lax.broadcasted_iota(jnp.int32, (tq, tq), 0)
    col = lax.broadcasted_iota(jnp.int32, (tq, tq), 1)
    if k_ref.shape[0] == tq:
        attend_at(0, col <= row)
    else:
        def full_block(j, carry):
            attend_at(pl.multiple_of(j * tq, tq), None)
            return carry

        lax.fori_loop(0, i, full_block, 0)
        attend_at(pl.multiple_of(i * tq, tq), col <= row)

    o_ref[...] = (acc_scr[...] / l_scr[...]).astype(o_ref.dtype)


def _attention(q, k, v, c, c_rows, prefix, *, tq):
    b, n, _ = q.shape
    q_spec = pl.BlockSpec((None, tq, HEAD_DIM), lambda bi, h, i: (bi, i, h))
    kv_spec = pl.BlockSpec((None, n, HEAD_DIM), lambda bi, h, i: (bi, 0, h))
    in_specs = [q_spec, kv_spec, kv_spec,
                pl.BlockSpec((None, tq, LANES), lambda bi, h, i: (bi, i, 0)),
                pl.BlockSpec((None, None, 1, n), lambda bi, h, i: (bi, h, 0, 0))]
    args = [q, k, v, c, c_rows]
    if prefix is not None:
        kp, vp, cp_rows = prefix
        n_p = kp.shape[0]
        in_specs += [pl.BlockSpec((n_p, HEAD_DIM), lambda bi, h, i: (0, h)),
                     pl.BlockSpec((n_p, HEAD_DIM), lambda bi, h, i: (0, h)),
                     pl.BlockSpec((None, 1, n_p), lambda bi, h, i: (h, 0, 0))]
        args += [kp, vp, cp_rows]
    return pl.pallas_call(
        functools.partial(_attention_kernel, has_prefix=prefix is not None),
        grid=(b, N_HEADS, n // tq),
        in_specs=in_specs,
        out_specs=q_spec,
        out_shape=jax.ShapeDtypeStruct(q.shape, BF16),
        scratch_shapes=[pltpu.VMEM((tq, 1), F32), pltpu.VMEM((tq, 1), F32), pltpu.VMEM((tq, HEAD_DIM), F32)],
        compiler_params=pltpu.CompilerParams(dimension_semantics=("arbitrary",) * 3,
                                             vmem_limit_bytes=VMEM_LIMIT),
        name="attention" if prefix is not None else "attention_meta",
    )(*args)


def _mixer_out_kernel(x_ref, att_ref, yc_ref, gmix_ref, gffn_ref, gfin_ref, wg_ref, woc_ref, woa_ref,
                      wo_ref, wa_ref, wv_ref, wd_ref, fcw_ref, halo_ref, out_ref, a_scr, *, tail_only):
    t = x_ref.shape[0]

    @pl.when(pl.program_id(1) == 0)
    def _():
        a_scr[0:HALO, :] = halo_ref[...]

    x = x_ref[...]
    h = _rms_norm(x, gmix_ref[...]).astype(BF16)
    g_att = _sigmoid(_dot(h, wg_ref[:, 0:D_MODEL]))
    g_conv = _sigmoid(_dot(h, wg_ref[:, D_MODEL:2 * D_MODEL]))
    mix = g_att * _dot(att_ref[...], woa_ref[...]) + g_conv * _dot(yc_ref[...], woc_ref[...])
    z1 = x + _dot(mix.astype(BF16), wo_ref[...])
    h2 = _rms_norm(z1, gffn_ref[...]).astype(BF16)

    if tail_only:
        a_scr[HALO:HALO + t, :] = _dot(h2, wa_ref[...])
        out_ref[...] = a_scr[t:t + HALO, :]
        return

    ffn = jnp.zeros((t, D_MODEL), F32)
    for lo, hi in FF_CHUNKS:
        a_scr[HALO:HALO + t, lo:hi] = _dot(h2, wa_ref[:, lo:hi])
        conv = _causal_conv(a_scr, lo, hi, t, fcw_ref)
        act = conv * _sigmoid(conv) * _dot(h2, wv_ref[:, lo:hi])
        ffn = ffn + _dot(act.astype(BF16), wd_ref[lo:hi, :])
    a_scr[0:HALO, :] = a_scr[t:t + HALO, :]
    out_ref[...] = _rms_norm(z1 + ffn, gfin_ref[...])


def _mixer_out(x, att, yc, g_mix, g_ffn, g_final, w_g, w_oc, w_oa, w_o, w_a, w_v, w_d, ffn_conv_w, halo,
               *, tile, tail_only):
    b, n, _ = x.shape
    rows = pl.BlockSpec((None, tile, D_MODEL), lambda bi, i: (bi, i, 0))
    if tail_only:
        assert b == 1 and n == tile
        out_shape = jax.ShapeDtypeStruct((HALO, D_FF), F32)
        out_spec = pl.BlockSpec((HALO, D_FF), lambda bi, i: (0, 0))
    else:
        out_shape = jax.ShapeDtypeStruct(x.shape, F32)
        out_spec = rows
    consts = (g_mix, g_ffn, g_final, w_g, w_oc, w_oa, w_o, w_a, w_v, w_d, ffn_conv_w, halo)
    return pl.pallas_call(
        functools.partial(_mixer_out_kernel, tail_only=tail_only),
        grid=(b, n // tile),
        in_specs=[rows, rows, rows] + [_resident(a.shape) for a in consts],
        out_specs=out_spec,
        out_shape=out_shape,
        scratch_shapes=[pltpu.VMEM((tile + HALO, D_FF), F32)],
        compiler_params=pltpu.CompilerParams(dimension_semantics=("arbitrary", "arbitrary"),
                                             vmem_limit_bytes=VMEM_LIMIT),
        name="mixer_out_meta" if tail_only else "mixer_out",
    )(x, att, yc, *consts)


def _head_rows(c):
    return jnp.transpose(c[:, :, :N_HEADS], (0, 2, 1))[:, :, None, :]


def kernel(x, meta_tokens, g_mix, w_in, b_f, conv_w, w_o_attn, w_o_conv, w_o, g_ffn, w_ffn_in, ffn_conv_w,
           w_ffn_out, g_final):
    assert w_in.shape[0] == 1, "one layer"
    d = D_MODEL
    w = w_in[0]
    w_qkv = w[:, 0:3 * d].astype(BF16)
    w_f = jnp.pad(w[:, 3 * d:3 * d + N_HEADS], ((0, 0), (0, LANES - N_HEADS))).astype(BF16)
    w_cv = w[:, 3 * d + N_HEADS:6 * d + N_HEADS].astype(BF16)
    w_g = w[:, 6 * d + N_HEADS:].astype(BF16)
    b_f_row = jnp.pad(b_f[0], (0, LANES - N_HEADS))[None, :]
    w_oa, w_oc, w_om = w_o_attn[0].astype(BF16), w_o_conv[0].astype(BF16), w_o[0].astype(BF16)
    w_a, w_v = w_ffn_in[0][:, :D_FF].astype(BF16), w_ffn_in[0][:, D_FF:].astype(BF16)
    w_d = w_ffn_out[0].astype(BF16)
    g_mix_row, g_ffn_row, g_final_row = g_mix[0][None, :], g_ffn[0][None, :], g_final[None, :]
    out_consts = (g_mix_row, g_ffn_row, g_final_row, w_g, w_oc, w_oa, w_om, w_a, w_v, w_d, ffn_conv_w[0])

    meta = meta_tokens[None].astype(F32)
    qm, km, vm, cm, ycm, cu_tail = _mixer_in(
        meta, g_mix_row, w_qkv, w_f, b_f_row, w_cv, conv_w[0],
        jnp.zeros((HALO, d), F32), jnp.zeros((1, LANES), F32), tile=N_META, emit_tail=True)
    cm_rows = _head_rows(cm)
    att_m = _attention(qm, km, vm, cm, cm_rows, None, tq=N_META)
    a_tail = _mixer_out(meta, att_m, ycm, *out_consts, jnp.zeros((HALO, D_FF), F32),
                        tile=N_META, tail_only=True)

    q, k, v, c, yc = _mixer_in(x, g_mix_row, w_qkv, w_f, b_f_row, w_cv, conv_w[0],
                               cu_tail, cm[0, N_META - 1:N_META, :], tile=ROW_TILE_IN, emit_tail=False)
    att = _attention(q, k, v, c, _head_rows(c), (km[0], vm[0], cm_rows[0]), tq=Q_TILE)
    return _mixer_out(x, att, yc, *out_consts, a_tail, tile=ROW_TILE_OUT, tail_only=False)
```

```python
import functools

import jax
import jax.numpy as jnp
from jax import lax
from jax.experimental import pallas as pl
from jax.experimental.pallas import tpu as pltpu

D_MODEL = 1024
N_HEADS = 8
HEAD_DIM = 128
N_META = 16
D_FF = 2816
RMS_EPS = 1e-6

LANES = 128
SUBLANES = 8
HALO = SUBLANES
VMEM_LIMIT = 56 * 1024 * 1024

LOG2E = 1.4426950408889634
Q_SCALE = HEAD_DIM ** -0.5 * LOG2E
MASKED = -1e30

META_TILE = LANES
ROW_TILE_IN = 512
ROW_TILE_OUT = 256
Q_TILE = 512
HEADS_PER_STEP = 4
FF_CHUNKS = ((0, 1536), (1536, D_FF))

BF16 = jnp.bfloat16
F32 = jnp.float32


def _dot(a, b):
    return jnp.dot(a, b, preferred_element_type=F32)


def _dot_nt(a, b):
    return lax.dot_general(a, b, (((1,), (1,)), ((), ())), preferred_element_type=F32)


def _rms_norm(x, g):
    return x * lax.rsqrt(jnp.mean(x * x, axis=-1, keepdims=True) + RMS_EPS) * g


def _sigmoid(x):
    return 1.0 / (1.0 + jnp.exp(-x))


def _split3(x):
    hi = x.astype(BF16).astype(F32)
    mid = (x - hi).astype(BF16).astype(F32)
    lo = (x - hi - mid).astype(BF16).astype(F32)
    return hi, mid, lo


def _cumsum_rows(x):
    t = x.shape[0]
    row = lax.broadcasted_iota(jnp.int32, (t, t), 0)
    col = lax.broadcasted_iota(jnp.int32, (t, t), 1)
    tri = jnp.where(col <= row, 1.0, 0.0).astype(BF16)
    hi, mid, lo = _split3(x)
    return _dot(tri, hi.astype(BF16)) + _dot(tri, mid.astype(BF16)) + _dot(tri, lo.astype(BF16))


def _causal_conv(scr, lo, hi, t, w_ref):
    out = scr[HALO - 2:HALO - 2 + t, lo:hi] * w_ref[0:1, lo:hi]
    out = out + scr[HALO - 1:HALO - 1 + t, lo:hi] * w_ref[1:2, lo:hi]
    return out + scr[HALO:HALO + t, lo:hi] * w_ref[2:3, lo:hi]


def _resident(shape):
    return pl.BlockSpec(shape, lambda *_: (0,) * len(shape), pipeline_mode=pl.Buffered(1))


def _mixer_in_kernel(x_ref, g_ref, wqkv_ref, wf_ref, bf_ref, wcv_ref, cw_ref, halo_ref, c0_ref,
                     q_ref, k_ref, vt_ref, c_ref, yc_ref, *rest, tail_at):
    if tail_at is not None:
        tail_ref, cu_scr, carry = rest
    else:
        cu_scr, carry = rest
    t = x_ref.shape[0]

    @pl.when(pl.program_id(1) == 0)
    def _():
        cu_scr[0:HALO, :] = halo_ref[...]
        carry[...] = c0_ref[...]

    h = _rms_norm(x_ref[...], g_ref[...]).astype(BF16)

    q_ref[...] = (_dot(h, wqkv_ref[:, 0:D_MODEL]) * Q_SCALE).astype(BF16)
    k_ref[...] = _dot(h, wqkv_ref[:, D_MODEL:2 * D_MODEL]).astype(BF16)
    vt_ref[...] = _dot(h, wqkv_ref[:, 2 * D_MODEL:3 * D_MODEL]).T.astype(BF16)

    f = _dot(h, wf_ref[...]) + bf_ref[...]
    log_f = (jnp.minimum(f, 0.0) - jnp.log1p(jnp.exp(-jnp.abs(f)))) * LOG2E
    c = _cumsum_rows(log_f) + carry[...]
    c_ref[...] = c
    carry[...] = c[t - 1:t, :]

    gb = _dot(h, wcv_ref[:, 0:D_MODEL])
    gc = _dot(h, wcv_ref[:, D_MODEL:2 * D_MODEL])
    u = _dot(h, wcv_ref[:, 2 * D_MODEL:3 * D_MODEL])
    cu_scr[HALO:HALO + t, :] = gc * u
    yc_ref[...] = (gb * _causal_conv(cu_scr, 0, D_MODEL, t, cw_ref)).astype(BF16)
    if tail_at is not None:
        tail_ref[...] = cu_scr[tail_at:tail_at + HALO, :]
    cu_scr[0:HALO, :] = cu_scr[t:t + HALO, :]


def _mixer_in(x, g_mix, w_qkv, w_f, b_f, w_cv, conv_w, halo, c0, *, tile, tail_at=None):
    b, n, _ = x.shape
    rows = pl.BlockSpec((None, tile, D_MODEL), lambda bi, i: (bi, i, 0))
    out_shape = [jax.ShapeDtypeStruct((b, n, D_MODEL), BF16), jax.ShapeDtypeStruct((b, n, D_MODEL), BF16),
                 jax.ShapeDtypeStruct((b, D_MODEL, n), BF16), jax.ShapeDtypeStruct((b, n, LANES), F32),
                 jax.ShapeDtypeStruct((b, n, D_MODEL), BF16)]
    out_specs = [rows, rows, pl.BlockSpec((None, D_MODEL, tile), lambda bi, i: (bi, 0, i)),
                 pl.BlockSpec((None, tile, LANES), lambda bi, i: (bi, i, 0)), rows]
    if tail_at is not None:
        assert b == 1 and n == tile
        out_shape.append(jax.ShapeDtypeStruct((HALO, D_MODEL), F32))
        out_specs.append(pl.BlockSpec((HALO, D_MODEL), lambda bi, i: (0, 0)))
    return pl.pallas_call(
        functools.partial(_mixer_in_kernel, tail_at=tail_at),
        grid=(b, n // tile),
        in_specs=[rows, _resident((1, D_MODEL)), _resident(w_qkv.shape), _resident(w_f.shape),
                  _resident(b_f.shape), _resident(w_cv.shape), _resident(conv_w.shape),
                  _resident(halo.shape), _resident(c0.shape)],
        out_specs=out_specs,
        out_shape=out_shape,
        scratch_shapes=[pltpu.VMEM((tile + HALO, D_MODEL), F32), pltpu.VMEM((1, LANES), F32)],
        compiler_params=pltpu.CompilerParams(dimension_semantics=("arbitrary", "arbitrary"),
                                             vmem_limit_bytes=VMEM_LIMIT),
        name="mixer_in" if tail_at is None else "mixer_in_meta",
    )(x, g_mix, w_qkv, w_f, b_f, w_cv, conv_w, halo, c0)


def _bias_columns(c, head, key_side):
    lane = lax.broadcasted_iota(jnp.int32, c.shape, 1)
    ch = jnp.sum(jnp.where(lane == head, c, 0.0), axis=1, keepdims=True)
    if key_side:
        ch = -ch
    pieces = _split3(ch)
    first, ones = (0, 3) if key_side else (3, 0)
    out = jnp.where((lane >= ones) & (lane < ones + 3), 1.0, 0.0)
    for n, piece in enumerate(pieces):
        out = jnp.where(lane == first + n, piece, out)
    return out.astype(BF16)


def _attention_kernel(q_ref, k_ref, vt_ref, c_ref, *rest, has_prefix):
    if has_prefix:
        kp_ref, vtp_ref, cp_ref, o_ref, kaug_scr, kpaug_scr, m_scr, l_scr, acc_scr = rest
    else:
        o_ref, kaug_scr, m_scr, l_scr, acc_scr = rest
    tq = q_ref.shape[0]
    n = k_ref.shape[0]
    group = range(q_ref.shape[1] // HEAD_DIM)
    head0 = pl.program_id(1) * len(group)
    i = pl.program_id(2)

    def cols(g):
        return slice(g * HEAD_DIM, (g + 1) * HEAD_DIM)

    @pl.when(i == 0)
    def _():
        def chunk(j, carry):
            off = pl.multiple_of(j * tq, tq)
            c_rows = c_ref[pl.ds(off, tq), :]
            for g in group:
                kaug_scr[g, pl.ds(off, tq), :] = _bias_columns(c_rows, head0 + g, True)
            return carry

        lax.fori_loop(0, n // tq, chunk, 0)
        if has_prefix:
            cp = cp_ref[...]
            prow = lax.broadcasted_iota(jnp.int32, cp.shape, 0)
            cp = jnp.where(prow < N_META, cp, -MASKED)
            for g in group:
                kpaug_scr[g] = _bias_columns(cp, head0 + g, True)

    off_q = pl.multiple_of(i * tq, tq)
    c_q = c_ref[pl.ds(off_q, tq), :]
    q_full = [jnp.concatenate([q_ref[:, cols(g)], _bias_columns(c_q, head0 + g, False)], axis=1) for g in group]

    m_scr[...] = jnp.full(m_scr.shape, MASKED, F32)
    l_scr[...] = jnp.zeros(l_scr.shape, F32)
    acc_scr[...] = jnp.zeros(acc_scr.shape, F32)

    def update(g, s, vt):
        m_old = m_scr[g]
        m_new = jnp.maximum(m_old, jnp.max(s, axis=0, keepdims=True))
        alpha = jnp.exp2(m_old - m_new)
        p = jnp.exp2(s - m_new)
        l_scr[g] = alpha * l_scr[g] + jnp.sum(p, axis=0, keepdims=True)
        acc_scr[g] = alpha * acc_scr[g] + _dot(vt, p.astype(BF16))
        m_scr[g] = m_new

    def scores(g, off):
        k_full = jnp.concatenate([k_ref[pl.ds(off, tq), cols(g)], kaug_scr[g, pl.ds(off, tq), :]], axis=1)
        return _dot_nt(k_full, q_full[g])

    def full_block(j, carry):
        off = pl.multiple_of(j * tq, tq)
        s = [scores(g, off) for g in group]
        for g in group:
            update(g, s[g], vt_ref[cols(g), pl.ds(off, tq)])
        return carry

    if n > tq:
        lax.fori_loop(0, i, full_block, 0)

    key = lax.broadcasted_iota(jnp.int32, (tq, tq), 0)
    qry = lax.broadcasted_iota(jnp.int32, (tq, tq), 1)
    for g in group:
        s = jnp.where(key <= qry, scores(g, off_q), MASKED)
        vt = vt_ref[cols(g), pl.ds(off_q, tq)]
        if has_prefix:
            kp_full = jnp.concatenate([kp_ref[:, cols(g)], kpaug_scr[g]], axis=1)
            s = jnp.concatenate([s, _dot_nt(kp_full, q_full[g])], axis=0)
            vt = jnp.concatenate([vt, vtp_ref[cols(g), :]], axis=1)
        update(g, s, vt)
        o_ref[:, cols(g)] = (acc_scr[g] / l_scr[g]).T.astype(o_ref.dtype)


def _attention(q, k, vt, c, prefix, *, tq, heads):
    b, n, _ = q.shape
    width = heads * HEAD_DIM
    q_spec = pl.BlockSpec((None, tq, width), lambda bi, h, i: (bi, i, h))
    in_specs = [q_spec,
                pl.BlockSpec((None, n, width), lambda bi, h, i: (bi, 0, h)),
                pl.BlockSpec((None, width, n), lambda bi, h, i: (bi, h, 0)),
                pl.BlockSpec((None, n, LANES), lambda bi, h, i: (bi, 0, 0))]
    args = [q, k, vt, c]
    scratch = [pltpu.VMEM((heads, n, LANES), BF16)]
    if prefix is not None:
        kp, vtp, cp = prefix
        n_p = kp.shape[0]
        in_specs += [pl.BlockSpec((n_p, width), lambda bi, h, i: (0, h)),
                     pl.BlockSpec((width, n_p), lambda bi, h, i: (h, 0)),
                     pl.BlockSpec((n_p, LANES), lambda bi, h, i: (0, 0))]
        args += [kp, vtp, cp]
        scratch.append(pltpu.VMEM((heads, n_p, LANES), BF16))
    scratch += [pltpu.VMEM((heads, 1, tq), F32), pltpu.VMEM((heads, 1, tq), F32),
                pltpu.VMEM((heads, HEAD_DIM, tq), F32)]
    return pl.pallas_call(
        functools.partial(_attention_kernel, has_prefix=prefix is not None),
        grid=(b, N_HEADS // heads, n // tq),
        in_specs=in_specs,
        out_specs=q_spec,
        out_shape=jax.ShapeDtypeStruct(q.shape, BF16),
        scratch_shapes=scratch,
        compiler_params=pltpu.CompilerParams(dimension_semantics=("arbitrary",) * 3,
                                             vmem_limit_bytes=VMEM_LIMIT),
        name="attention" if prefix is not None else "attention_meta",
    )(*args)


def _mixer_out_kernel(x_ref, att_ref, yc_ref, gmix_ref, gffn_ref, gfin_ref, wg_ref, woc_ref, woa_ref,
                      wo_ref, wa_ref, wv_ref, wd_ref, fcw_ref, halo_ref, out_ref, a_scr, *, tail_at):
    t = x_ref.shape[0]

    @pl.when(pl.program_id(1) == 0)
    def _():
        a_scr[0:HALO, :] = halo_ref[...]

    x = x_ref[...]
    h = _rms_norm(x, gmix_ref[...]).astype(BF16)
    g_att = _sigmoid(_dot(h, wg_ref[:, 0:D_MODEL]))
    g_conv = _sigmoid(_dot(h, wg_ref[:, D_MODEL:2 * D_MODEL]))
    mix = g_att * _dot(att_ref[...], woa_ref[...]) + g_conv * _dot(yc_ref[...], woc_ref[...])
    z1 = x + _dot(mix.astype(BF16), wo_ref[...])
    h2 = _rms_norm(z1, gffn_ref[...]).astype(BF16)

    if tail_at is not None:
        a_scr[HALO:HALO + t, :] = _dot(h2, wa_ref[...])
        out_ref[...] = a_scr[tail_at:tail_at + HALO, :]
        return

    ffn = jnp.zeros((t, D_MODEL), F32)
    for lo, hi in FF_CHUNKS:
        a_scr[HALO:HALO + t, lo:hi] = _dot(h2, wa_ref[:, lo:hi])
        conv = _causal_conv(a_scr, lo, hi, t, fcw_ref)
        act = conv * _sigmoid(conv) * _dot(h2, wv_ref[:, lo:hi])
        ffn = ffn + _dot(act.astype(BF16), wd_ref[lo:hi, :])
    a_scr[0:HALO, :] = a_scr[t:t + HALO, :]
    out_ref[...] = _rms_norm(z1 + ffn, gfin_ref[...])


def _mixer_out(x, att, yc, g_mix, g_ffn, g_final, w_g, w_oc, w_oa, w_o, w_a, w_v, w_d, ffn_conv_w, halo,
               *, tile, tail_at=None):
    b, n, _ = x.shape
    rows = pl.BlockSpec((None, tile, D_MODEL), lambda bi, i: (bi, i, 0))
    if tail_at is not None:
        assert b == 1 and n == tile
        out_shape = jax.ShapeDtypeStruct((HALO, D_FF), F32)
        out_spec = pl.BlockSpec((HALO, D_FF), lambda bi, i: (0, 0))
    else:
        out_shape = jax.ShapeDtypeStruct(x.shape, F32)
        out_spec = rows
    consts = (g_mix, g_ffn, g_final, w_g, w_oc, w_oa, w_o, w_a, w_v, w_d, ffn_conv_w, halo)
    return pl.pallas_call(
        functools.partial(_mixer_out_kernel, tail_at=tail_at),
        grid=(b, n // tile),
        in_specs=[rows, rows, rows] + [_resident(a.shape) for a in consts],
        out_specs=out_spec,
        out_shape=out_shape,
        scratch_shapes=[pltpu.VMEM((tile + HALO, D_FF), F32)],
        compiler_params=pltpu.CompilerParams(dimension_semantics=("arbitrary", "arbitrary"),
                                             vmem_limit_bytes=VMEM_LIMIT),
        name="mixer_out" if tail_at is None else "mixer_out_meta",
    )(x, att, yc, *consts)


def kernel(x, meta_tokens, g_mix, w_in, b_f, conv_w, w_o_attn, w_o_conv, w_o, g_ffn, w_ffn_in, ffn_conv_w,
           w_ffn_out, g_final):
    assert w_in.shape[0] == 1, "one layer"
    d = D_MODEL
    w = w_in[0]
    w_qkv = w[:, 0:3 * d].astype(BF16)
    w_f = jnp.pad(w[:, 3 * d:3 * d + N_HEADS], ((0, 0), (0, LANES - N_HEADS))).astype(BF16)
    w_cv = w[:, 3 * d + N_HEADS:6 * d + N_HEADS].astype(BF16)
    w_g = w[:, 6 * d + N_HEADS:].astype(BF16)
    b_f_row = jnp.pad(b_f[0], (0, LANES - N_HEADS))[None, :]
    w_oa, w_oc, w_om = w_o_attn[0].astype(BF16), w_o_conv[0].astype(BF16), w_o[0].astype(BF16)
    w_a, w_v = w_ffn_in[0][:, :D_FF].astype(BF16), w_ffn_in[0][:, D_FF:].astype(BF16)
    w_d = w_ffn_out[0].astype(BF16)
    g_mix_row, g_ffn_row, g_final_row = g_mix[0][None, :], g_ffn[0][None, :], g_final[None, :]
    in_consts = (g_mix_row, w_qkv, w_f, b_f_row, w_cv, conv_w[0])
    out_consts = (g_mix_row, g_ffn_row, g_final_row, w_g, w_oc, w_oa, w_om, w_a, w_v, w_d, ffn_conv_w[0])

    meta = jnp.pad(meta_tokens.astype(F32), ((0, META_TILE - N_META), (0, 0)))[None]
    qm, km, vtm, cm, ycm, cu_tail = _mixer_in(
        meta, *in_consts, jnp.zeros((HALO, d), F32), jnp.zeros((1, LANES), F32), tile=META_TILE, tail_at=N_META)
    att_m = _attention(qm, km, vtm, cm, None, tq=META_TILE, heads=HEADS_PER_STEP)
    a_tail = _mixer_out(meta, att_m, ycm, *out_consts, jnp.zeros((HALO, D_FF), F32),
                        tile=META_TILE, tail_at=N_META)

    q, k, vt, c, yc = _mixer_in(x, *in_consts, cu_tail, cm[0, N_META - 1:N_META, :], tile=ROW_TILE_IN)
    att = _attention(q, k, vt, c, (km[0], vtm[0], cm[0]), tq=Q_TILE, heads=HEADS_PER_STEP)
    return _mixer_out(x, att, yc, *out_consts, a_tail, tile=ROW_TILE_OUT)
```

```python
import functools

import jax
import jax.numpy as jnp
from jax import lax
from jax.experimental import pallas as pl
from jax.experimental.pallas import tpu as pltpu

D_MODEL = 1024
N_HEADS = 8
HEAD_DIM = 128
N_META = 16
D_FF = 2816
RMS_EPS = 1e-6

LANES = 128
SUBLANES = 8
HALO = SUBLANES
VMEM_LIMIT = 56 * 1024 * 1024

LOG2E = 1.4426950408889634
Q_SCALE = HEAD_DIM ** -0.5 * LOG2E
MASKED = -1e30

META_TILE = LANES
ROW_TILE_IN = 512
ROW_TILE_OUT = 256
Q_TILE = 512
HEADS_PER_STEP = 4
FF_CHUNKS = ((0, 1536), (1536, D_FF))

BF16 = jnp.bfloat16
F32 = jnp.float32


def _dot(a, b):
    return jnp.dot(a, b, preferred_element_type=F32)


def _dot_nt(a, b):
    return lax.dot_general(a, b, (((1,), (1,)), ((), ())), preferred_element_type=F32)


def _rms_norm(x, g):
    return x * lax.rsqrt(jnp.mean(x * x, axis=-1, keepdims=True) + RMS_EPS) * g


def _sigmoid(x):
    return 1.0 / (1.0 + jnp.exp(-x))


def _split3(x):
    hi = x.astype(BF16).astype(F32)
    mid = (x - hi).astype(BF16).astype(F32)
    lo = (x - hi - mid).astype(BF16).astype(F32)
    return hi, mid, lo


def _cumsum_rows(x):
    t = x.shape[0]
    row = lax.broadcasted_iota(jnp.int32, (t, t), 0)
    col = lax.broadcasted_iota(jnp.int32, (t, t), 1)
    tri = jnp.where(col <= row, 1.0, 0.0).astype(BF16)
    hi, mid, lo = _split3(x)
    return _dot(tri, hi.astype(BF16)) + _dot(tri, mid.astype(BF16)) + _dot(tri, lo.astype(BF16))


def _causal_conv(scr, lo, hi, t, w_ref):
    out = scr[HALO - 2:HALO - 2 + t, lo:hi] * w_ref[0:1, lo:hi]
    out = out + scr[HALO - 1:HALO - 1 + t, lo:hi] * w_ref[1:2, lo:hi]
    return out + scr[HALO:HALO + t, lo:hi] * w_ref[2:3, lo:hi]


def _resident(shape):
    return pl.BlockSpec(shape, lambda *_: (0,) * len(shape), pipeline_mode=pl.Buffered(1))


def _mixer_in_kernel(x_ref, g_ref, wqkv_ref, wf_ref, bf_ref, wcv_ref, cw_ref, halo_ref, c0_ref,
                     q_ref, k_ref, vt_ref, c_ref, yc_ref, *rest, tail_at):
    if tail_at is not None:
        tail_ref, cu_scr, carry = rest
    else:
        cu_scr, carry = rest
    t = x_ref.shape[0]

    @pl.when(pl.program_id(1) == 0)
    def _():
        cu_scr[0:HALO, :] = halo_ref[...]
        carry[...] = c0_ref[...]

    h = _rms_norm(x_ref[...], g_ref[...]).astype(BF16)

    q_ref[...] = (_dot(h, wqkv_ref[:, 0:D_MODEL]) * Q_SCALE).astype(BF16)
    k_ref[...] = _dot(h, wqkv_ref[:, D_MODEL:2 * D_MODEL]).astype(BF16)
    vt_ref[...] = _dot(h, wqkv_ref[:, 2 * D_MODEL:3 * D_MODEL]).T.astype(BF16)

    f = _dot(h, wf_ref[...]) + bf_ref[...]
    log_f = (jnp.minimum(f, 0.0) - jnp.log1p(jnp.exp(-jnp.abs(f)))) * LOG2E
    c = _cumsum_rows(log_f) + carry[...]
    c_ref[...] = c
    carry[...] = c[t - 1:t, :]

    gb = _dot(h, wcv_ref[:, 0:D_MODEL])
    gc = _dot(h, wcv_ref[:, D_MODEL:2 * D_MODEL])
    u = _dot(h, wcv_ref[:, 2 * D_MODEL:3 * D_MODEL])
    cu_scr[HALO:HALO + t, :] = gc * u
    yc_ref[...] = (gb * _causal_conv(cu_scr, 0, D_MODEL, t, cw_ref)).astype(BF16)
    if tail_at is not None:
        tail_ref[...] = cu_scr[tail_at:tail_at + HALO, :]
    cu_scr[0:HALO, :] = cu_scr[t:t + HALO, :]


def _mixer_in(x, g_mix, w_qkv, w_f, b_f, w_cv, conv_w, halo, c0, *, tile, tail_at=None):
    b, n, _ = x.shape
    rows = pl.BlockSpec((None, tile, D_MODEL), lambda bi, i: (bi, i, 0))
    out_shape = [jax.ShapeDtypeStruct((b, n, D_MODEL), BF16), jax.ShapeDtypeStruct((b, n, D_MODEL), BF16),
                 jax.ShapeDtypeStruct((b, D_MODEL, n), BF16), jax.ShapeDtypeStruct((b, n, LANES), F32),
                 jax.ShapeDtypeStruct((b, n, D_MODEL), BF16)]
    out_specs = [rows, rows, pl.BlockSpec((None, D_MODEL, tile), lambda bi, i: (bi, 0, i)),
                 pl.BlockSpec((None, tile, LANES), lambda bi, i: (bi, i, 0)), rows]
    if tail_at is not None:
        assert b == 1 and n == tile
        out_shape.append(jax.ShapeDtypeStruct((HALO, D_MODEL), F32))
        out_specs.append(pl.BlockSpec((HALO, D_MODEL), lambda bi, i: (0, 0)))
    return pl.pallas_call(
        functools.partial(_mixer_in_kernel, tail_at=tail_at),
        grid=(b, n // tile),
        in_specs=[rows, _resident((1, D_MODEL)), _resident(w_qkv.shape), _resident(w_f.shape),
                  _resident(b_f.shape), _resident(w_cv.shape), _resident(conv_w.shape),
                  _resident(halo.shape), _resident(c0.shape)],
        out_specs=out_specs,
        out_shape=out_shape,
        scratch_shapes=[pltpu.VMEM((tile + HALO, D_MODEL), F32), pltpu.VMEM((1, LANES), F32)],
        compiler_params=pltpu.CompilerParams(dimension_semantics=("arbitrary", "arbitrary"),
                                             vmem_limit_bytes=VMEM_LIMIT),
        name="mixer_in" if tail_at is None else "mixer_in_meta",
    )(x, g_mix, w_qkv, w_f, b_f, w_cv, conv_w, halo, c0)


def _bias_columns(c, head, key_side):
    lane = lax.broadcasted_iota(jnp.int32, c.shape, 1)
    ch = jnp.sum(jnp.where(lane == head, c, 0.0), axis=1, keepdims=True)
    if key_side:
        ch = -ch
    pieces = _split3(ch)
    first, ones = (0, 3) if key_side else (3, 0)
    out = jnp.where((lane >= ones) & (lane < ones + 3), 1.0, 0.0)
    for n, piece in enumerate(pieces):
        out = jnp.where(lane == first + n, piece, out)
    return out.astype(BF16)


def _attention_kernel(q_ref, k_ref, vt_ref, c_ref, *rest, has_prefix):
    if has_prefix:
        kp_ref, vtp_ref, cp_ref, o_ref, kaug_scr, kpaug_scr, s_scr, s2_scr, m_scr, l_scr, acc_scr = rest
    else:
        o_ref, kaug_scr, s_scr, s2_scr, m_scr, l_scr, acc_scr = rest
    tq = q_ref.shape[0]
    n = k_ref.shape[0]
    group = range(q_ref.shape[1] // HEAD_DIM)
    head0 = pl.program_id(1) * len(group)
    i = pl.program_id(2)

    def cols(g):
        return slice(g * HEAD_DIM, (g + 1) * HEAD_DIM)

    @pl.when(i == 0)
    def _():
        def chunk(j, carry):
            off = pl.multiple_of(j * tq, tq)
            c_rows = c_ref[pl.ds(off, tq), :]
            for g in group:
                kaug_scr[g, pl.ds(off, tq), :] = _bias_columns(c_rows, head0 + g, True)
            return carry

        lax.fori_loop(0, n // tq, chunk, 0)
        if has_prefix:
            cp = cp_ref[...]
            prow = lax.broadcasted_iota(jnp.int32, cp.shape, 0)
            cp = jnp.where(prow < N_META, cp, -MASKED)
            for g in group:
                kpaug_scr[g] = _bias_columns(cp, head0 + g, True)

    off_q = pl.multiple_of(i * tq, tq)
    c_q = c_ref[pl.ds(off_q, tq), :]
    q_full = [jnp.concatenate([q_ref[:, cols(g)], _bias_columns(c_q, head0 + g, False)], axis=1) for g in group]

    m_scr[...] = jnp.full(m_scr.shape, MASKED, F32)
    l_scr[...] = jnp.zeros(l_scr.shape, F32)
    acc_scr[...] = jnp.zeros(acc_scr.shape, F32)

    def update(g, s, vt):
        m_old = m_scr[g]
        m_new = jnp.maximum(m_old, jnp.max(s, axis=0, keepdims=True))
        alpha = jnp.exp2(m_old - m_new)
        p = jnp.exp2(s - m_new)
        l_scr[g] = alpha * l_scr[g] + jnp.sum(p, axis=0, keepdims=True)
        acc_scr[g] = alpha * acc_scr[g] + _dot(vt, p.astype(BF16))
        m_scr[g] = m_new

    def scores(g, off):
        k_full = jnp.concatenate([k_ref[pl.ds(off, tq), cols(g)], kaug_scr[g, pl.ds(off, tq), :]], axis=1)
        return _dot_nt(k_full, q_full[g])

    def step(j, src, dst):
        off = pl.multiple_of(j * tq, tq)
        off_next = pl.multiple_of(off + tq, tq)
        for g in group:
            dst[g] = scores(g, off_next)
        for g in group:
            update(g, src[g], vt_ref[cols(g), pl.ds(off, tq)])

    def pair(jj, carry):
        j = (i & 1) + 2 * jj
        step(j, s_scr, s2_scr)
        step(j + 1, s2_scr, s_scr)
        return carry

    for g in group:
        s_scr[g] = scores(g, 0)
    if n > tq:
        @pl.when((i & 1) == 1)
        def _():
            step(0, s_scr, s2_scr)
            s_scr[...] = s2_scr[...]

        lax.fori_loop(0, lax.shift_right_logical(i, 1), pair, 0)

    key = lax.broadcasted_iota(jnp.int32, (tq, tq), 0)
    qry = lax.broadcasted_iota(jnp.int32, (tq, tq), 1)
    for g in group:
        s = jnp.where(key <= qry, s_scr[g], MASKED)
        vt = vt_ref[cols(g), pl.ds(off_q, tq)]
        if has_prefix:
            kp_full = jnp.concatenate([kp_ref[:, cols(g)], kpaug_scr[g]], axis=1)
            s = jnp.concatenate([s, _dot_nt(kp_full, q_full[g])], axis=0)
            vt = jnp.concatenate([vt, vtp_ref[cols(g), :]], axis=1)
        update(g, s, vt)
        o_ref[:, cols(g)] = (acc_scr[g] / l_scr[g]).T.astype(o_ref.dtype)


def _attention(q, k, vt, c, prefix, *, tq, heads):
    b, n, _ = q.shape
    width = heads * HEAD_DIM
    q_spec = pl.BlockSpec((None, tq, width), lambda bi, h, i: (bi, i, h))
    in_specs = [q_spec,
                pl.BlockSpec((None, n, width), lambda bi, h, i: (bi, 0, h)),
                pl.BlockSpec((None, width, n), lambda bi, h, i: (bi, h, 0)),
                pl.BlockSpec((None, n, LANES), lambda bi, h, i: (bi, 0, 0))]
    args = [q, k, vt, c]
    scratch = [pltpu.VMEM((heads, n, LANES), BF16)]
    if prefix is not None:
        kp, vtp, cp = prefix
        n_p = kp.shape[0]
        in_specs += [pl.BlockSpec((n_p, width), lambda bi, h, i: (0, h)),
                     pl.BlockSpec((width, n_p), lambda bi, h, i: (h, 0)),
                     pl.BlockSpec((n_p, LANES), lambda bi, h, i: (0, 0))]
        args += [kp, vtp, cp]
        scratch.append(pltpu.VMEM((heads, n_p, LANES), BF16))
    scratch += [pltpu.VMEM((heads, tq, tq), F32), pltpu.VMEM((heads, tq, tq), F32),
                pltpu.VMEM((heads, 1, tq), F32), pltpu.VMEM((heads, 1, tq), F32),
                pltpu.VMEM((heads, HEAD_DIM, tq), F32)]
    return pl.pallas_call(
        functools.partial(_attention_kernel, has_prefix=prefix is not None),
        grid=(b, N_HEADS // heads, n // tq),
        in_specs=in_specs,
        out_specs=q_spec,
        out_shape=jax.ShapeDtypeStruct(q.shape, BF16),
        scratch_shapes=scratch,
        compiler_params=pltpu.CompilerParams(dimension_semantics=("arbitrary",) * 3,
                                             vmem_limit_bytes=VMEM_LIMIT),
        name="attention" if prefix is not None else "attention_meta",
    )(*args)


def _mixer_out_kernel(x_ref, att_ref, yc_ref, gmix_ref, gffn_ref, gfin_ref, wg_ref, woc_ref, woa_ref,
                      wo_ref, wa_ref, wv_ref, wd_ref, fcw_ref, halo_ref, out_ref, a_scr, *, tail_at):
    t = x_ref.shape[0]

    @pl.when(pl.program_id(1) == 0)
    def _():
        a_scr[0:HALO, :] = halo_ref[...]

    x = x_ref[...]
    h = _rms_norm(x, gmix_ref[...]).astype(BF16)
    g_att = _sigmoid(_dot(h, wg_ref[:, 0:D_MODEL]))
    g_conv = _sigmoid(_dot(h, wg_ref[:, D_MODEL:2 * D_MODEL]))
    mix = g_att * _dot(att_ref[...], woa_ref[...]) + g_conv * _dot(yc_ref[...], woc_ref[...])
    z1 = x + _dot(mix.astype(BF16), wo_ref[...])
    h2 = _rms_norm(z1, gffn_ref[...]).astype(BF16)

    if tail_at is not None:
        a_scr[HALO:HALO + t, :] = _dot(h2, wa_ref[...])
        out_ref[...] = a_scr[tail_at:tail_at + HALO, :]
        return

    ffn = jnp.zeros((t, D_MODEL), F32)
    for lo, hi in FF_CHUNKS:
        a_scr[HALO:HALO + t, lo:hi] = _dot(h2, wa_ref[:, lo:hi])
        conv = _causal_conv(a_scr, lo, hi, t, fcw_ref)
        act = conv * _sigmoid(conv) * _dot(h2, wv_ref[:, lo:hi])
        ffn = ffn + _dot(act.astype(BF16), wd_ref[lo:hi, :])
    a_scr[0:HALO, :] = a_scr[t:t + HALO, :]
    out_ref[...] = _rms_norm(z1 + ffn, gfin_ref[...])


def _mixer_out(x, att, yc, g_mix, g_ffn, g_final, w_g, w_oc, w_oa, w_o, w_a, w_v, w_d, ffn_conv_w, halo,
               *, tile, tail_at=None):
    b, n, _ = x.shape
    rows = pl.BlockSpec((None, tile, D_MODEL), lambda bi, i: (bi, i, 0))
    if tail_at is not None:
        assert b == 1 and n == tile
        out_shape = jax.ShapeDtypeStruct((HALO, D_FF), F32)
        out_spec = pl.BlockSpec((HALO, D_FF), lambda bi, i: (0, 0))
    else:
        out_shape = jax.ShapeDtypeStruct(x.shape, F32)
        out_spec = rows
    consts = (g_mix, g_ffn, g_final, w_g, w_oc, w_oa, w_o, w_a, w_v, w_d, ffn_conv_w, halo)
    return pl.pallas_call(
        functools.partial(_mixer_out_kernel, tail_at=tail_at),
        grid=(b, n // tile),
        in_specs=[rows, rows, rows] + [_resident(a.shape) for a in consts],
        out_specs=out_spec,
        out_shape=out_shape,
        scratch_shapes=[pltpu.VMEM((tile + HALO, D_FF), F32)],
        compiler_params=pltpu.CompilerParams(dimension_semantics=("arbitrary", "arbitrary"),
                                             vmem_limit_bytes=VMEM_LIMIT),
        name="mixer_out" if tail_at is None else "mixer_out_meta",
    )(x, att, yc, *consts)


def kernel(x, meta_tokens, g_mix, w_in, b_f, conv_w, w_o_attn, w_o_conv, w_o, g_ffn, w_ffn_in, ffn_conv_w,
           w_ffn_out, g_final):
    assert w_in.shape[0] == 1, "one layer"
    d = D_MODEL
    w = w_in[0]
    w_qkv = w[:, 0:3 * d].astype(BF16)
    w_f = jnp.pad(w[:, 3 * d:3 * d + N_HEADS], ((0, 0), (0, LANES - N_HEADS))).astype(BF16)
    w_cv = w[:, 3 * d + N_HEADS:6 * d + N_HEADS].astype(BF16)
    w_g = w[:, 6 * d + N_HEADS:].astype(BF16)
    b_f_row = jnp.pad(b_f[0], (0, LANES - N_HEADS))[None, :]
    w_oa, w_oc, w_om = w_o_attn[0].astype(BF16), w_o_conv[0].astype(BF16), w_o[0].astype(BF16)
    w_a, w_v = w_ffn_in[0][:, :D_FF].astype(BF16), w_ffn_in[0][:, D_FF:].astype(BF16)
    w_d = w_ffn_out[0].astype(BF16)
    g_mix_row, g_ffn_row, g_final_row = g_mix[0][None, :], g_ffn[0][None, :], g_final[None, :]
    in_consts = (g_mix_row, w_qkv, w_f, b_f_row, w_cv, conv_w[0])
    out_consts = (g_mix_row, g_ffn_row, g_final_row, w_g, w_oc, w_oa, w_om, w_a, w_v, w_d, ffn_conv_w[0])

    meta = jnp.pad(meta_tokens.astype(F32), ((0, META_TILE - N_META), (0, 0)))[None]
    qm, km, vtm, cm, ycm, cu_tail = _mixer_in(
        meta, *in_consts, jnp.zeros((HALO, d), F32), jnp.zeros((1, LANES), F32), tile=META_TILE, tail_at=N_META)
    att_m = _attention(qm, km, vtm, cm, None, tq=META_TILE, heads=HEADS_PER_STEP)
    a_tail = _mixer_out(meta, att_m, ycm, *out_consts, jnp.zeros((HALO, D_FF), F32),
                        tile=META_TILE, tail_at=N_META)

    q, k, vt, c, yc = _mixer_in(x, *in_consts, cu_tail, cm[0, N_META - 1:N_META, :], tile=ROW_TILE_IN)
    att = _attention(q, k, vt, c, (km[0], vtm[0], cm[0]), tq=Q_TILE, heads=HEADS_PER_STEP)
    return _mixer_out(x, att, yc, *out_consts, a_tail, tile=ROW_TILE_OUT)
```

```python
import functools

import jax
import jax.numpy as jnp
from jax import lax
from jax.experimental import pallas as pl
from jax.experimental.pallas import tpu as pltpu

D_MODEL = 1024
N_HEADS = 8
HEAD_DIM = 128
N_META = 16
D_FF = 2816
RMS_EPS = 1e-6

LANES = 128
SUBLANES = 8
BF16_ROWS = 2 * SUBLANES
HALO = SUBLANES
VMEM_LIMIT = 56 * 1024 * 1024

LOG2E = 1.4426950408889634
Q_SCALE = HEAD_DIM ** -0.5 * LOG2E
MASKED = -1e30

META_TILE = LANES
ROW_TILE_IN = 512
ROW_TILE_OUT = 512
Q_TILE = 512
HEADS_PER_STEP = 4
FF_CHUNKS = ((0, 1536), (1536, D_FF))

BF16 = jnp.bfloat16
F32 = jnp.float32


def _dot(a, b):
    return jnp.dot(a, b, preferred_element_type=F32)


def _dot_nt(a, b):
    return lax.dot_general(a, b, (((1,), (1,)), ((), ())), preferred_element_type=F32)


def _rms_norm(x, g):
    return x * lax.rsqrt(jnp.mean(x * x, axis=-1, keepdims=True) + RMS_EPS) * g


def _sigmoid(x):
    return 1.0 / (1.0 + jnp.exp(-x))


def _split3(x):
    hi = x.astype(BF16).astype(F32)
    mid = (x - hi).astype(BF16).astype(F32)
    lo = (x - hi - mid).astype(BF16).astype(F32)
    return hi, mid, lo


def _cumsum_rows(x, start):
    t = x.shape[0]
    row = lax.broadcasted_iota(jnp.int32, (LANES, LANES), 0)
    col = lax.broadcasted_iota(jnp.int32, (LANES, LANES), 1)
    tri = jnp.where(col <= row, 1.0, 0.0).astype(BF16)
    blocks = []
    for r in range(0, t, LANES):
        hi, mid, lo = _split3(x[r:r + LANES])
        local = _dot(tri, hi.astype(BF16)) + _dot(tri, mid.astype(BF16)) + _dot(tri, lo.astype(BF16))
        blocks.append(local + start)
        start = blocks[-1][LANES - 1:LANES]
    return jnp.concatenate(blocks, axis=0), start


def _causal_conv(scr, lo, hi, t, w_ref):
    out = scr[HALO - 2:HALO - 2 + t, lo:hi] * w_ref[0:1, lo:hi]
    out = out + scr[HALO - 1:HALO - 1 + t, lo:hi] * w_ref[1:2, lo:hi]
    return out + scr[HALO:HALO + t, lo:hi] * w_ref[2:3, lo:hi]


def _resident(shape):
    return pl.BlockSpec(shape, lambda *_: (0,) * len(shape), pipeline_mode=pl.Buffered(1))


def _mixer_in_kernel(x_ref, g_ref, wqkv_ref, wf_ref, bf_ref, wcv_ref, cw_ref, halo_ref, c0_ref,
                     q_ref, k_ref, vt_ref, c_ref, yc_ref, *rest, tail_at):
    if tail_at is not None:
        tail_ref, cu_scr, carry = rest
    else:
        cu_scr, carry = rest
    t = x_ref.shape[0]

    @pl.when(pl.program_id(1) == 0)
    def _():
        cu_scr[0:HALO, :] = halo_ref[...]
        carry[...] = c0_ref[...]

    h = _rms_norm(x_ref[...], g_ref[...]).astype(BF16)

    gc = _dot(h, wcv_ref[:, D_MODEL:2 * D_MODEL])
    u = _dot(h, wcv_ref[:, 2 * D_MODEL:3 * D_MODEL])
    cu_scr[HALO:HALO + t, :] = gc * u
    conv = _causal_conv(cu_scr, 0, D_MODEL, t, cw_ref)
    if tail_at is not None:
        tail_ref[...] = cu_scr[tail_at:tail_at + HALO, :]
    cu_scr[0:HALO, :] = cu_scr[t:t + HALO, :]

    f = _dot(h, wf_ref[...]) + bf_ref[...]
    log_f = (jnp.minimum(f, 0.0) - jnp.log1p(jnp.exp(-jnp.abs(f)))) * LOG2E
    c_ref[...], carry[...] = _cumsum_rows(log_f, carry[...])

    vt_ref[...] = _dot(h, wqkv_ref[:, 2 * D_MODEL:3 * D_MODEL]).T.astype(BF16)
    q_ref[...] = (_dot(h, wqkv_ref[:, 0:D_MODEL]) * Q_SCALE).astype(BF16)
    k_ref[...] = _dot(h, wqkv_ref[:, D_MODEL:2 * D_MODEL]).astype(BF16)
    yc_ref[...] = (_dot(h, wcv_ref[:, 0:D_MODEL]) * conv).astype(BF16)


def _mixer_in(x, g_mix, w_qkv, w_f, b_f, w_cv, conv_w, halo, c0, *, tile, tail_at=None):
    b, n, _ = x.shape
    rows = pl.BlockSpec((None, tile, D_MODEL), lambda bi, i: (bi, i, 0))
    out_shape = [jax.ShapeDtypeStruct((b, n, D_MODEL), BF16), jax.ShapeDtypeStruct((b, n, D_MODEL), BF16),
                 jax.ShapeDtypeStruct((b, D_MODEL, n), BF16), jax.ShapeDtypeStruct((b, n, LANES), F32),
                 jax.ShapeDtypeStruct((b, n, D_MODEL), BF16)]
    out_specs = [rows, rows, pl.BlockSpec((None, D_MODEL, tile), lambda bi, i: (bi, 0, i)),
                 pl.BlockSpec((None, tile, LANES), lambda bi, i: (bi, i, 0)), rows]
    if tail_at is not None:
        assert b == 1 and n == tile
        out_shape.append(jax.ShapeDtypeStruct((HALO, D_MODEL), F32))
        out_specs.append(pl.BlockSpec((HALO, D_MODEL), lambda bi, i: (0, 0)))
    return pl.pallas_call(
        functools.partial(_mixer_in_kernel, tail_at=tail_at),
        grid=(b, n // tile),
        in_specs=[rows, _resident((1, D_MODEL)), _resident(w_qkv.shape), _resident(w_f.shape),
                  _resident(b_f.shape), _resident(w_cv.shape), _resident(conv_w.shape),
                  _resident(halo.shape), _resident(c0.shape)],
        out_specs=out_specs,
        out_shape=out_shape,
        scratch_shapes=[pltpu.VMEM((tile + HALO, D_MODEL), F32), pltpu.VMEM((1, LANES), F32)],
        compiler_params=pltpu.CompilerParams(dimension_semantics=("arbitrary", "arbitrary"),
                                             vmem_limit_bytes=VMEM_LIMIT),
        name="mixer_in" if tail_at is None else "mixer_in_meta",
    )(x, g_mix, w_qkv, w_f, b_f, w_cv, conv_w, halo, c0)


def _bias_columns(c, head, key_side):
    lane = lax.broadcasted_iota(jnp.int32, c.shape, 1)
    ch = jnp.sum(jnp.where(lane == head, c, 0.0), axis=1, keepdims=True)
    if key_side:
        ch = -ch
    pieces = _split3(ch)
    first, ones = (0, 3) if key_side else (3, 0)
    out = jnp.where((lane >= ones) & (lane < ones + 3), 1.0, 0.0)
    for n, piece in enumerate(pieces):
        out = jnp.where(lane == first + n, piece, out)
    return out.astype(BF16)


def _attention_kernel(q_ref, k_ref, vt_ref, c_ref, *rest, has_prefix):
    if has_prefix:
        kp_ref, vtp_ref, cp_ref, o_ref, kaug_scr, kpaug_scr, s_scr, s2_scr, m_scr, acc_scr = rest
    else:
        o_ref, kaug_scr, s_scr, s2_scr, m_scr, acc_scr = rest
    tq = q_ref.shape[0]
    n = k_ref.shape[0]
    group = range(q_ref.shape[1] // HEAD_DIM)
    head0 = pl.program_id(1) * len(group)
    i = pl.program_id(2)

    def cols(g):
        return slice(g * HEAD_DIM, (g + 1) * HEAD_DIM)

    @pl.when(i == 0)
    def _():
        def chunk(j, carry):
            off = pl.multiple_of(j * tq, tq)
            c_rows = c_ref[pl.ds(off, tq), :]
            for g in group:
                kaug_scr[g, pl.ds(off, tq), :] = _bias_columns(c_rows, head0 + g, True)
            return carry

        lax.fori_loop(0, n // tq, chunk, 0)
        if has_prefix:
            cp = cp_ref[...]
            prow = lax.broadcasted_iota(jnp.int32, cp.shape, 0)
            cp = jnp.where(prow < N_META, cp, -MASKED)
            for g in group:
                kpaug_scr[g] = _bias_columns(cp, head0 + g, True)

    off_q = pl.multiple_of(i * tq, tq)
    c_q = c_ref[pl.ds(off_q, tq), :]
    q_full = [jnp.concatenate([q_ref[:, cols(g)], _bias_columns(c_q, head0 + g, False)], axis=1) for g in group]

    m_scr[...] = jnp.full(m_scr.shape, MASKED, F32)
    acc_scr[...] = jnp.zeros(acc_scr.shape, F32)

    def ones_rows(keys):
        first = lax.broadcasted_iota(jnp.int32, (BF16_ROWS, keys), 0) == 0
        return jnp.where(first, 1.0, 0.0).astype(BF16)

    ones_tq = ones_rows(tq)

    def update(g, s, vt_ones):
        m_old = m_scr[g]
        m_new = jnp.maximum(m_old, jnp.max(s, axis=0, keepdims=True))
        alpha = jnp.exp2(m_old - m_new)
        p = jnp.exp2(s - m_new).astype(BF16)
        acc_scr[g] = alpha * acc_scr[g] + _dot(vt_ones, p)
        m_scr[g] = m_new

    def scores(g, off):
        k_full = jnp.concatenate([k_ref[pl.ds(off, tq), cols(g)], kaug_scr[g, pl.ds(off, tq), :]], axis=1)
        return _dot_nt(k_full, q_full[g])

    def step(j, src, dst):
        off = pl.multiple_of(j * tq, tq)
        off_next = pl.multiple_of(off + tq, tq)
        for g in group:
            dst[g] = scores(g, off_next)
        for g in group:
            update(g, src[g], jnp.concatenate([vt_ref[cols(g), pl.ds(off, tq)], ones_tq], axis=0))

    def pair(jj, carry):
        j = (i & 1) + 2 * jj
        step(j, s_scr, s2_scr)
        step(j + 1, s2_scr, s_scr)
        return carry

    for g in group:
        s_scr[g] = scores(g, 0)
    if n > tq:
        @pl.when((i & 1) == 1)
        def _():
            step(0, s_scr, s2_scr)
            s_scr[...] = s2_scr[...]

        lax.fori_loop(0, lax.shift_right_logical(i, 1), pair, 0)

    key = lax.broadcasted_iota(jnp.int32, (tq, tq), 0)
    qry = lax.broadcasted_iota(jnp.int32, (tq, tq), 1)
    n_p = kp_ref.shape[0] if has_prefix else 0
    ones_diag = ones_rows(tq + n_p)
    for g in group:
        s = jnp.where(key <= qry, s_scr[g], MASKED)
        vt = vt_ref[cols(g), pl.ds(off_q, tq)]
        if has_prefix:
            kp_full = jnp.concatenate([kp_ref[:, cols(g)], kpaug_scr[g]], axis=1)
            s = jnp.concatenate([s, _dot_nt(kp_full, q_full[g])], axis=0)
            vt = jnp.concatenate([vt, vtp_ref[cols(g), :]], axis=1)
        update(g, s, jnp.concatenate([vt, ones_diag], axis=0))
        acc = acc_scr[g]
        inv_l = 1.0 / acc[HEAD_DIM:HEAD_DIM + 1]
        o_ref[:, cols(g)] = (acc[0:HEAD_DIM] * inv_l).T.astype(o_ref.dtype)


def _attention(q, k, vt, c, prefix, *, tq, heads):
    b, n, _ = q.shape
    width = heads * HEAD_DIM
    q_spec = pl.BlockSpec((None, tq, width), lambda bi, h, i: (bi, i, h))
    in_specs = [q_spec,
                pl.BlockSpec((None, n, width), lambda bi, h, i: (bi, 0, h)),
                pl.BlockSpec((None, width, n), lambda bi, h, i: (bi, h, 0)),
                pl.BlockSpec((None, n, LANES), lambda bi, h, i: (bi, 0, 0))]
    args = [q, k, vt, c]
    scratch = [pltpu.VMEM((heads, n, LANES), BF16)]
    if prefix is not None:
        kp, vtp, cp = prefix
        n_p = kp.shape[0]
        in_specs += [pl.BlockSpec((n_p, width), lambda bi, h, i: (0, h)),
                     pl.BlockSpec((width, n_p), lambda bi, h, i: (h, 0)),
                     pl.BlockSpec((n_p, LANES), lambda bi, h, i: (0, 0))]
        args += [kp, vtp, cp]
        scratch.append(pltpu.VMEM((heads, n_p, LANES), BF16))
    scratch += [pltpu.VMEM((heads, tq, tq), F32), pltpu.VMEM((heads, tq, tq), F32),
                pltpu.VMEM((heads, 1, tq), F32),
                pltpu.VMEM((heads, HEAD_DIM + BF16_ROWS, tq), F32)]
    return pl.pallas_call(
        functools.partial(_attention_kernel, has_prefix=prefix is not None),
        grid=(b, N_HEADS // heads, n // tq),
        in_specs=in_specs,
        out_specs=q_spec,
        out_shape=jax.ShapeDtypeStruct(q.shape, BF16),
        scratch_shapes=scratch,
        compiler_params=pltpu.CompilerParams(dimension_semantics=("arbitrary",) * 3,
                                             vmem_limit_bytes=VMEM_LIMIT),
        name="attention" if prefix is not None else "attention_meta",
    )(*args)


def _mixer_out_kernel(x_ref, att_ref, yc_ref, gmix_ref, gffn_ref, gfin_ref, wg_ref, woc_ref, woa_ref,
                      wo_ref, wa_ref, wv_ref, wd_ref, fcw_ref, halo_ref, out_ref, a_scr, *, tail_at):
    t = x_ref.shape[0]

    @pl.when(pl.program_id(1) == 0)
    def _():
        a_scr[0:HALO, :] = halo_ref[...]

    x = x_ref[...]
    h = _rms_norm(x, gmix_ref[...]).astype(BF16)
    g_att = _sigmoid(_dot(h, wg_ref[:, 0:D_MODEL]))
    g_conv = _sigmoid(_dot(h, wg_ref[:, D_MODEL:2 * D_MODEL]))
    mix = g_att * _dot(att_ref[...], woa_ref[...]) + g_conv * _dot(yc_ref[...], woc_ref[...])
    z1 = x + _dot(mix.astype(BF16), wo_ref[...])
    h2 = _rms_norm(z1, gffn_ref[...]).astype(BF16)

    if tail_at is not None:
        a_scr[HALO:HALO + t, :] = _dot(h2, wa_ref[...])
        out_ref[...] = a_scr[tail_at:tail_at + HALO, :]
        return

    ffn = jnp.zeros((t, D_MODEL), F32)
    for lo, hi in FF_CHUNKS:
        a_scr[HALO:HALO + t, lo:hi] = _dot(h2, wa_ref[:, lo:hi])
        conv = _causal_conv(a_scr, lo, hi, t, fcw_ref)
        act = conv * _sigmoid(conv) * _dot(h2, wv_ref[:, lo:hi])
        ffn = ffn + _dot(act.astype(BF16), wd_ref[lo:hi, :])
    a_scr[0:HALO, :] = a_scr[t:t + HALO, :]
    out_ref[...] = _rms_norm(z1 + ffn, gfin_ref[...])


def _mixer_out(x, att, yc, g_mix, g_ffn, g_final, w_g, w_oc, w_oa, w_o, w_a, w_v, w_d, ffn_conv_w, halo,
               *, tile, tail_at=None):
    b, n, _ = x.shape
    rows = pl.BlockSpec((None, tile, D_MODEL), lambda bi, i: (bi, i, 0))
    if tail_at is not None:
        assert b == 1 and n == tile
        out_shape = jax.ShapeDtypeStruct((HALO, D_FF), F32)
        out_spec = pl.BlockSpec((HALO, D_FF), lambda bi, i: (0, 0))
    else:
        out_shape = jax.ShapeDtypeStruct(x.shape, F32)
        out_spec = rows
    consts = (g_mix, g_ffn, g_final, w_g, w_oc, w_oa, w_o, w_a, w_v, w_d, ffn_conv_w, halo)
    return pl.pallas_call(
        functools.partial(_mixer_out_kernel, tail_at=tail_at),
        grid=(b, n // tile),
        in_specs=[rows, rows, rows] + [_resident(a.shape) for a in consts],
        out_specs=out_spec,
        out_shape=out_shape,
        scratch_shapes=[pltpu.VMEM((tile + HALO, D_FF), F32)],
        compiler_params=pltpu.CompilerParams(dimension_semantics=("arbitrary", "arbitrary"),
                                             vmem_limit_bytes=VMEM_LIMIT),
        name="mixer_out" if tail_at is None else "mixer_out_meta",
    )(x, att, yc, *consts)


def kernel(x, meta_tokens, g_mix, w_in, b_f, conv_w, w_o_attn, w_o_conv, w_o, g_ffn, w_ffn_in, ffn_conv_w,
           w_ffn_out, g_final):
    assert w_in.shape[0] == 1, "one layer"
    d = D_MODEL
    w = w_in[0]
    w_qkv = w[:, 0:3 * d].astype(BF16)
    w_f = jnp.pad(w[:, 3 * d:3 * d + N_HEADS], ((0, 0), (0, LANES - N_HEADS))).astype(BF16)
    w_cv = w[:, 3 * d + N_HEADS:6 * d + N_HEADS].astype(BF16)
    w_g = w[:, 6 * d + N_HEADS:].astype(BF16)
    b_f_row = jnp.pad(b_f[0], (0, LANES - N_HEADS))[None, :]
    w_oa, w_oc, w_om = w_o_attn[0].astype(BF16), w_o_conv[0].astype(BF16), w_o[0].astype(BF16)
    w_a, w_v = w_ffn_in[0][:, :D_FF].astype(BF16), w_ffn_in[0][:, D_FF:].astype(BF16)
    w_d = w_ffn_out[0].astype(BF16)
    g_mix_row, g_ffn_row, g_final_row = g_mix[0][None, :], g_ffn[0][None, :], g_final[None, :]
    in_consts = (g_mix_row, w_qkv, w_f, b_f_row, w_cv, conv_w[0])
    out_consts = (g_mix_row, g_ffn_row, g_final_row, w_g, w_oc, w_oa, w_om, w_a, w_v, w_d, ffn_conv_w[0])

    meta = jnp.pad(meta_tokens.astype(F32), ((0, META_TILE - N_META), (0, 0)))[None]
    qm, km, vtm, cm, ycm, cu_tail = _mixer_in(
        meta, *in_consts, jnp.zeros((HALO, d), F32), jnp.zeros((1, LANES), F32), tile=META_TILE, tail_at=N_META)
    att_m = _attention(qm, km, vtm, cm, None, tq=META_TILE, heads=HEADS_PER_STEP)
    a_tail = _mixer_out(meta, att_m, ycm, *out_consts, jnp.zeros((HALO, D_FF), F32),
                        tile=META_TILE, tail_at=N_META)

    q, k, vt, c, yc = _mixer_in(x, *in_consts, cu_tail, cm[0, N_META - 1:N_META, :], tile=ROW_TILE_IN)
    att = _attention(q, k, vt, c, (km[0], vtm[0], cm[0]), tq=Q_TILE, heads=HEADS_PER_STEP)
    return _mixer_out(x, att, yc, *out_consts, a_tail, tile=ROW_TILE_OUT)
```

```python
import functools

import jax
import jax.numpy as jnp
from jax import lax
from jax.experimental import pallas as pl
from jax.experimental.pallas import tpu as pltpu

D_MODEL = 1024
N_HEADS = 8
HEAD_DIM = 128
N_META = 16
D_FF = 2816
RMS_EPS = 1e-6

LANES = 128
SUBLANES = 8
BF16_ROWS = 2 * SUBLANES
HALO = SUBLANES
VMEM_LIMIT = 56 * 1024 * 1024

LOG2E = 1.4426950408889634
Q_SCALE = HEAD_DIM ** -0.5 * LOG2E
MASKED = -1e30

META_TILE = LANES
ROW_TILE_IN = 512
ROW_TILE_OUT = 512
Q_TILE = 512
HEADS_PER_STEP = 4
FF_CHUNKS = ((0, 1536), (1536, D_FF))

BF16 = jnp.bfloat16
F32 = jnp.float32


def _dot(a, b):
    return jnp.dot(a, b, preferred_element_type=F32)


def _dot_nt(a, b):
    return lax.dot_general(a, b, (((1,), (1,)), ((), ())), preferred_element_type=F32)


def _rms_norm(x, g):
    return x * lax.rsqrt(jnp.mean(x * x, axis=-1, keepdims=True) + RMS_EPS) * g


def _sigmoid(x):
    return 1.0 / (1.0 + jnp.exp(-x))


def _split3(x):
    hi = x.astype(BF16).astype(F32)
    mid = (x - hi).astype(BF16).astype(F32)
    lo = (x - hi - mid).astype(BF16).astype(F32)
    return hi, mid, lo


def _cumsum_rows(x, start):
    t = x.shape[0]
    row = lax.broadcasted_iota(jnp.int32, (LANES, LANES), 0)
    col = lax.broadcasted_iota(jnp.int32, (LANES, LANES), 1)
    tri = jnp.where(col <= row, 1.0, 0.0).astype(BF16)
    blocks = []
    for r in range(0, t, LANES):
        hi, mid, lo = _split3(x[r:r + LANES])
        local = _dot(tri, hi.astype(BF16)) + _dot(tri, mid.astype(BF16)) + _dot(tri, lo.astype(BF16))
        blocks.append(local + start)
        start = blocks[-1][LANES - 1:LANES]
    return jnp.concatenate(blocks, axis=0), start


def _causal_conv(scr, lo, hi, t, w_ref):
    out = scr[HALO - 2:HALO - 2 + t, lo:hi] * w_ref[0:1, lo:hi]
    out = out + scr[HALO - 1:HALO - 1 + t, lo:hi] * w_ref[1:2, lo:hi]
    return out + scr[HALO:HALO + t, lo:hi] * w_ref[2:3, lo:hi]


def _resident(shape):
    return pl.BlockSpec(shape, lambda *_: (0,) * len(shape), pipeline_mode=pl.Buffered(1))


def _bias_columns(c, head):
    lane = lax.broadcasted_iota(jnp.int32, c.shape, 1)
    pieces = _split3(c[:, head:head + 1])
    k_side = jnp.where((lane >= 3) & (lane < 6), 1.0, 0.0)
    q_side = jnp.where(lane < 3, 1.0, 0.0)
    for n, piece in enumerate(pieces):
        k_side = jnp.where(lane == n, -piece, k_side)
        q_side = jnp.where(lane == 3 + n, piece, q_side)
    return k_side.astype(BF16), q_side.astype(BF16)


def _mixer_in_kernel(x_ref, g_ref, wqkv_ref, wf_ref, bf_ref, wcv_ref, cw_ref, halo_ref, c0_ref,
                     qf_ref, kf_ref, vt_ref, yc_ref, *rest, tail_at):
    if tail_at is not None:
        c_ref, tail_ref, cu_scr, carry = rest
    else:
        cu_scr, carry = rest
    t = x_ref.shape[0]

    @pl.when(pl.program_id(1) == 0)
    def _():
        cu_scr[0:HALO, :] = halo_ref[...]
        carry[...] = c0_ref[...]

    h = _rms_norm(x_ref[...], g_ref[...]).astype(BF16)

    gc = _dot(h, wcv_ref[:, D_MODEL:2 * D_MODEL])
    u = _dot(h, wcv_ref[:, 2 * D_MODEL:3 * D_MODEL])
    cu_scr[HALO:HALO + t, :] = gc * u
    conv = _causal_conv(cu_scr, 0, D_MODEL, t, cw_ref)
    if tail_at is not None:
        tail_ref[...] = cu_scr[tail_at:tail_at + HALO, :]
    cu_scr[0:HALO, :] = cu_scr[t:t + HALO, :]

    f = _dot(h, wf_ref[...]) + bf_ref[...]
    log_f = (jnp.minimum(f, 0.0) - jnp.log1p(jnp.exp(-jnp.abs(f)))) * LOG2E
    c, carry[...] = _cumsum_rows(log_f, carry[...])
    if tail_at is not None:
        c_ref[...] = c
        c = jnp.where(lax.broadcasted_iota(jnp.int32, c.shape, 0) < tail_at, c, -MASKED)
    for head in range(N_HEADS):
        lo = head * 2 * HEAD_DIM + HEAD_DIM
        kf_ref[:, lo:lo + HEAD_DIM], qf_ref[:, lo:lo + HEAD_DIM] = _bias_columns(c, head)

    vt_ref[...] = _dot(h, wqkv_ref[:, 2 * D_MODEL:3 * D_MODEL]).T.astype(BF16)
    q = (_dot(h, wqkv_ref[:, 0:D_MODEL]) * Q_SCALE).astype(BF16)
    k = _dot(h, wqkv_ref[:, D_MODEL:2 * D_MODEL]).astype(BF16)
    for head in range(N_HEADS):
        src = slice(head * HEAD_DIM, (head + 1) * HEAD_DIM)
        dst = slice(head * 2 * HEAD_DIM, head * 2 * HEAD_DIM + HEAD_DIM)
        qf_ref[:, dst] = q[:, src]
        kf_ref[:, dst] = k[:, src]
    yc_ref[...] = (_dot(h, wcv_ref[:, 0:D_MODEL]) * conv).astype(BF16)


def _mixer_in(x, g_mix, w_qkv, w_f, b_f, w_cv, conv_w, halo, c0, *, tile, tail_at=None):
    b, n, _ = x.shape
    rows = pl.BlockSpec((None, tile, D_MODEL), lambda bi, i: (bi, i, 0))
    wide = pl.BlockSpec((None, tile, 2 * D_MODEL), lambda bi, i: (bi, i, 0))
    out_shape = [jax.ShapeDtypeStruct((b, n, 2 * D_MODEL), BF16), jax.ShapeDtypeStruct((b, n, 2 * D_MODEL), BF16),
                 jax.ShapeDtypeStruct((b, D_MODEL, n), BF16), jax.ShapeDtypeStruct((b, n, D_MODEL), BF16)]
    out_specs = [wide, wide, pl.BlockSpec((None, D_MODEL, tile), lambda bi, i: (bi, 0, i)), rows]
    if tail_at is not None:
        assert b == 1 and n == tile
        out_shape += [jax.ShapeDtypeStruct((b, n, LANES), F32), jax.ShapeDtypeStruct((HALO, D_MODEL), F32)]
        out_specs += [pl.BlockSpec((None, tile, LANES), lambda bi, i: (bi, i, 0)),
                      pl.BlockSpec((HALO, D_MODEL), lambda bi, i: (0, 0))]
    return pl.pallas_call(
        functools.partial(_mixer_in_kernel, tail_at=tail_at),
        grid=(b, n // tile),
        in_specs=[rows, _resident((1, D_MODEL)), _resident(w_qkv.shape), _resident(w_f.shape),
                  _resident(b_f.shape), _resident(w_cv.shape), _resident(conv_w.shape),
                  _resident(halo.shape), _resident(c0.shape)],
        out_specs=out_specs,
        out_shape=out_shape,
        scratch_shapes=[pltpu.VMEM((tile + HALO, D_MODEL), F32), pltpu.VMEM((1, LANES), F32)],
        compiler_params=pltpu.CompilerParams(dimension_semantics=("arbitrary", "arbitrary"),
                                             vmem_limit_bytes=VMEM_LIMIT),
        name="mixer_in" if tail_at is None else "mixer_in_meta",
    )(x, g_mix, w_qkv, w_f, b_f, w_cv, conv_w, halo, c0)


def _attention_kernel(qf_ref, *rest, has_prefix, multi):
    rest = list(rest)
    qfn_ref = rest.pop(0) if multi else None
    kf_ref, vt_ref = rest.pop(0), rest.pop(0)
    kpf_ref, vtp_ref = (rest.pop(0), rest.pop(0)) if has_prefix else (None, None)
    o_ref, diag_scr, ab_scr, m_scr, acc_scr = rest
    tq = qf_ref.shape[0]
    n_q = kf_ref.shape[0] // tq
    group = range(vt_ref.shape[0] // HEAD_DIM)
    i = pl.program_id(2)

    def cols(g):
        return slice(g * HEAD_DIM, (g + 1) * HEAD_DIM)

    def wide(g):
        return slice(g * 2 * HEAD_DIM, (g + 1) * 2 * HEAD_DIM)

    def rows_at(block):
        return pl.ds(pl.multiple_of(block * tq, tq), tq)

    def ones_rows(keys):
        first_row = lax.broadcasted_iota(jnp.int32, (BF16_ROWS, keys), 0) == 0
        return jnp.where(first_row, 1.0, 0.0).astype(BF16)

    key = lax.broadcasted_iota(jnp.int32, (tq, tq), 0)
    qry = lax.broadcasted_iota(jnp.int32, (tq, tq), 1)
    ones_tq = ones_rows(tq)
    ones_diag = ones_rows(tq + (kpf_ref.shape[0] if has_prefix else 0))
    q_cur = [qf_ref[:, wide(g)] for g in group]

    def produce(dst, block, q):
        for g in group:
            dst[g] = _dot_nt(kf_ref[rows_at(block), wide(g)], q[g])

    def update(g, s, vt_ones):
        m_old = m_scr[g]
        m_new = jnp.maximum(m_old, jnp.max(s, axis=0, keepdims=True))
        alpha = jnp.exp2(m_old - m_new)
        p = jnp.exp2(s - m_new).astype(BF16)
        acc_scr[g] = alpha * acc_scr[g] + _dot(vt_ones, p)
        m_scr[g] = m_new

    def prefix_scores():
        return [_dot_nt(kpf_ref[:, wide(g)], q_cur[g]) for g in group] if has_prefix else None

    def consume(src, block, s_prefix=None, diagonal=False):
        for g in group:
            s, vt, ones = src[g], vt_ref[cols(g), rows_at(block)], ones_tq
            if diagonal:
                s, ones = jnp.where(key <= qry, s, MASKED), ones_diag
                if has_prefix:
                    s = jnp.concatenate([s, s_prefix[g]], axis=0)
                    vt = jnp.concatenate([vt, vtp_ref[cols(g), :]], axis=1)
            update(g, s, jnp.concatenate([vt, ones], axis=0))

    @pl.when(i == 0)
    def _():
        produce(diag_scr, 0, q_cur)

    m_scr[...] = jnp.full(m_scr.shape, MASKED, F32)
    acc_scr[...] = jnp.zeros(acc_scr.shape, F32)

    if not multi:
        consume(diag_scr, 0, prefix_scores(), diagonal=True)
    else:
        first = i == 0
        nxt = jnp.minimum(i + 1, n_q - 1)
        q_nxt = [qfn_ref[:, wide(g)] for g in group]

        s_prefix = prefix_scores()
        produce(ab_scr.at[jnp.where(first, 0, (i - 1) & 1)], jnp.where(first, nxt, 0),
                [jnp.where(first, q_nxt[g], q_cur[g]) for g in group])
        consume(diag_scr, i, s_prefix, diagonal=True)

        @pl.when(first)
        def _():
            diag_scr[...] = ab_scr[0]

        def step(j, src, dst):
            produce(dst, j + 1, q_cur)
            consume(src, j)

        @pl.when((i >= 2) & ((i & 1) == 0))
        def _():
            step(0, ab_scr.at[1], ab_scr.at[0])

        def pair(jj, carry):
            j = 1 - (i & 1) + 2 * jj
            step(j, ab_scr.at[0], ab_scr.at[1])
            step(j + 1, ab_scr.at[1], ab_scr.at[0])
            return carry

        lax.fori_loop(0, lax.shift_right_logical(jnp.maximum(i - 1, 0), 1), pair, 0)

        @pl.when(i >= 1)
        def _():
            produce(diag_scr, nxt, q_nxt)
            consume(ab_scr.at[0], i - 1)

    for g in group:
        acc = acc_scr[g]
        inv_l = 1.0 / acc[HEAD_DIM:HEAD_DIM + 1]
        o_ref[:, cols(g)] = (acc[0:HEAD_DIM] * inv_l).T.astype(o_ref.dtype)


def _attention(qf, kf, vt, prefix, *, tq, heads):
    b, n, _ = qf.shape
    n_q = n // tq
    width = heads * HEAD_DIM
    q_spec = pl.BlockSpec((None, tq, 2 * width), lambda bi, h, i: (bi, i, h))
    in_specs, args = [q_spec], [qf]
    if n_q > 1:
        in_specs.append(pl.BlockSpec((None, tq, 2 * width), lambda bi, h, i: (bi, jnp.minimum(i + 1, n_q - 1), h)))
        args.append(qf)
    in_specs += [pl.BlockSpec((None, n, 2 * width), lambda bi, h, i: (bi, 0, h)),
                 pl.BlockSpec((None, width, n), lambda bi, h, i: (bi, h, 0))]
    args += [kf, vt]
    if prefix is not None:
        kpf, vtp = prefix
        n_p = kpf.shape[0]
        in_specs += [pl.BlockSpec((n_p, 2 * width), lambda bi, h, i: (0, h)),
                     pl.BlockSpec((width, n_p), lambda bi, h, i: (h, 0))]
        args += [kpf, vtp]
    scratch = [pltpu.VMEM((heads, tq, tq), F32), pltpu.VMEM((2, heads, tq, tq), F32),
               pltpu.VMEM((heads, 1, tq), F32),
               pltpu.VMEM((heads, HEAD_DIM + BF16_ROWS, tq), F32)]
    return pl.pallas_call(
        functools.partial(_attention_kernel, has_prefix=prefix is not None, multi=n_q > 1),
        grid=(b, N_HEADS // heads, n_q),
        in_specs=in_specs,
        out_specs=pl.BlockSpec((None, tq, width), lambda bi, h, i: (bi, i, h)),
        out_shape=jax.ShapeDtypeStruct((b, n, D_MODEL), BF16),
        scratch_shapes=scratch,
        compiler_params=pltpu.CompilerParams(dimension_semantics=("arbitrary",) * 3,
                                             vmem_limit_bytes=VMEM_LIMIT),
        name="attention" if prefix is not None else "attention_meta",
    )(*args)


def _mixer_out_kernel(x_ref, att_ref, yc_ref, gmix_ref, gffn_ref, gfin_ref, wg_ref, woc_ref, woa_ref,
                      wo_ref, wa_ref, wv_ref, wd_ref, fcw_ref, halo_ref, out_ref, a_scr, *, tail_at):
    t = x_ref.shape[0]

    @pl.when(pl.program_id(1) == 0)
    def _():
        a_scr[0:HALO, :] = halo_ref[...]

    x = x_ref[...]
    h = _rms_norm(x, gmix_ref[...]).astype(BF16)
    g_att = _sigmoid(_dot(h, wg_ref[:, 0:D_MODEL]))
    g_conv = _sigmoid(_dot(h, wg_ref[:, D_MODEL:2 * D_MODEL]))
    mix = g_att * _dot(att_ref[...], woa_ref[...]) + g_conv * _dot(yc_ref[...], woc_ref[...])
    z1 = x + _dot(mix.astype(BF16), wo_ref[...])
    h2 = _rms_norm(z1, gffn_ref[...]).astype(BF16)

    if tail_at is not None:
        a_scr[HALO:HALO + t, :] = _dot(h2, wa_ref[...])
        out_ref[...] = a_scr[tail_at:tail_at + HALO, :]
        return

    ffn = jnp.zeros((t, D_MODEL), F32)
    for lo, hi in FF_CHUNKS:
        a_scr[HALO:HALO + t, lo:hi] = _dot(h2, wa_ref[:, lo:hi])
        conv = _causal_conv(a_scr, lo, hi, t, fcw_ref)
        act = conv * _sigmoid(conv) * _dot(h2, wv_ref[:, lo:hi])
        ffn = ffn + _dot(act.astype(BF16), wd_ref[lo:hi, :])
    a_scr[0:HALO, :] = a_scr[t:t + HALO, :]
    out_ref[...] = _rms_norm(z1 + ffn, gfin_ref[...])


def _mixer_out(x, att, yc, g_mix, g_ffn, g_final, w_g, w_oc, w_oa, w_o, w_a, w_v, w_d, ffn_conv_w, halo,
               *, tile, tail_at=None):
    b, n, _ = x.shape
    rows = pl.BlockSpec((None, tile, D_MODEL), lambda bi, i: (bi, i, 0))
    if tail_at is not None:
        assert b == 1 and n == tile
        out_shape = jax.ShapeDtypeStruct((HALO, D_FF), F32)
        out_spec = pl.BlockSpec((HALO, D_FF), lambda bi, i: (0, 0))
    else:
        out_shape = jax.ShapeDtypeStruct(x.shape, F32)
        out_spec = rows
    consts = (g_mix, g_ffn, g_final, w_g, w_oc, w_oa, w_o, w_a, w_v, w_d, ffn_conv_w, halo)
    return pl.pallas_call(
        functools.partial(_mixer_out_kernel, tail_at=tail_at),
        grid=(b, n // tile),
        in_specs=[rows, rows, rows] + [_resident(a.shape) for a in consts],
        out_specs=out_spec,
        out_shape=out_shape,
        scratch_shapes=[pltpu.VMEM((tile + HALO, D_FF), F32)],
        compiler_params=pltpu.CompilerParams(dimension_semantics=("arbitrary", "arbitrary"),
                                             vmem_limit_bytes=VMEM_LIMIT),
        name="mixer_out" if tail_at is None else "mixer_out_meta",
    )(x, att, yc, *consts)


def kernel(x, meta_tokens, g_mix, w_in, b_f, conv_w, w_o_attn, w_o_conv, w_o, g_ffn, w_ffn_in, ffn_conv_w,
           w_ffn_out, g_final):
    assert w_in.shape[0] == 1, "one layer"
    d = D_MODEL
    w = w_in[0]
    w_qkv = w[:, 0:3 * d].astype(BF16)
    w_f = jnp.pad(w[:, 3 * d:3 * d + N_HEADS], ((0, 0), (0, LANES - N_HEADS))).astype(BF16)
    w_cv = w[:, 3 * d + N_HEADS:6 * d + N_HEADS].astype(BF16)
    w_g = w[:, 6 * d + N_HEADS:].astype(BF16)
    b_f_row = jnp.pad(b_f[0], (0, LANES - N_HEADS))[None, :]
    w_oa, w_oc, w_om = w_o_attn[0].astype(BF16), w_o_conv[0].astype(BF16), w_o[0].astype(BF16)
    w_a, w_v = w_ffn_in[0][:, :D_FF].astype(BF16), w_ffn_in[0][:, D_FF:].astype(BF16)
    w_d = w_ffn_out[0].astype(BF16)
    g_mix_row, g_ffn_row, g_final_row = g_mix[0][None, :], g_ffn[0][None, :], g_final[None, :]
    in_consts = (g_mix_row, w_qkv, w_f, b_f_row, w_cv, conv_w[0])
    out_consts = (g_mix_row, g_ffn_row, g_final_row, w_g, w_oc, w_oa, w_om, w_a, w_v, w_d, ffn_conv_w[0])

    meta = jnp.pad(meta_tokens.astype(F32), ((0, META_TILE - N_META), (0, 0)))[None]
    qfm, kfm, vtm, ycm, cm, cu_tail = _mixer_in(
        meta, *in_consts, jnp.zeros((HALO, d), F32), jnp.zeros((1, LANES), F32), tile=META_TILE, tail_at=N_META)
    att_m = _attention(qfm, kfm, vtm, None, tq=META_TILE, heads=HEADS_PER_STEP)
    a_tail = _mixer_out(meta, att_m, ycm, *out_consts, jnp.zeros((HALO, D_FF), F32),
                        tile=META_TILE, tail_at=N_META)

    qf, kf, vt, yc = _mixer_in(x, *in_consts, cu_tail, cm[0, N_META - 1:N_META, :], tile=ROW_TILE_IN)
    att = _attention(qf, kf, vt, (kfm[0], vtm[0]), tq=Q_TILE, heads=HEADS_PER_STEP)
    return _mixer_out(x, att, yc, *out_consts, a_tail, tile=ROW_TILE_OUT)
```

```python
import functools

import jax
import jax.numpy as jnp
from jax import lax
from jax.experimental import pallas as pl
from jax.experimental.pallas import tpu as pltpu

D_MODEL = 1024
N_HEADS = 8
HEAD_DIM = 128
N_META = 16
D_FF = 2816
RMS_EPS = 1e-6

LANES = 128
SUBLANES = 8
BF16_ROWS = 2 * SUBLANES
HALO = SUBLANES
VMEM_LIMIT = 56 * 1024 * 1024

LOG2E = 1.4426950408889634
Q_SCALE = HEAD_DIM ** -0.5 * LOG2E
MASKED = -1e30

META_TILE = LANES
ROW_TILE_IN = 512
ROW_TILE_OUT = 512
Q_TILE = 512
HEADS_PER_STEP = 4
FF_CHUNKS = ((0, 1536), (1536, D_FF))
W_IN_BLOCKS = {
    "qkv": ((D_MODEL, 3 * D_MODEL), (0, 0)),
    "conv": ((D_MODEL, 3 * D_MODEL), (0, 1)),
    "gates": ((D_MODEL, 2 * D_MODEL), (0, 3)),
    "f": ((D_MODEL, LANES), (0, 8 * D_MODEL // LANES)),
}

BF16 = jnp.bfloat16
F32 = jnp.float32


def _dot(a, b):
    return jnp.dot(a, b, preferred_element_type=F32)


def _dot_nt(a, b):
    return lax.dot_general(a, b, (((1,), (1,)), ((), ())), preferred_element_type=F32)


def _rms_norm(x, g):
    return x * lax.rsqrt(jnp.mean(x * x, axis=-1, keepdims=True) + RMS_EPS) * g


def _sigmoid(x):
    return 1.0 / (1.0 + jnp.exp(-x))


def _split3(x):
    hi = x.astype(BF16).astype(F32)
    mid = (x - hi).astype(BF16).astype(F32)
    lo = (x - hi - mid).astype(BF16).astype(F32)
    return hi, mid, lo


def _cumsum_rows(x, start):
    t = x.shape[0]
    row = lax.broadcasted_iota(jnp.int32, (LANES, LANES), 0)
    col = lax.broadcasted_iota(jnp.int32, (LANES, LANES), 1)
    tri = jnp.where(col <= row, 1.0, 0.0).astype(BF16)
    blocks = []
    for r in range(0, t, LANES):
        hi, mid, lo = _split3(x[r:r + LANES])
        local = _dot(tri, hi.astype(BF16)) + _dot(tri, mid.astype(BF16)) + _dot(tri, lo.astype(BF16))
        blocks.append(local + start)
        start = blocks[-1][LANES - 1:LANES]
    return jnp.concatenate(blocks, axis=0), start


def _causal_conv(scr, lo, hi, t, w_ref):
    out = scr[HALO - 2:HALO - 2 + t, lo:hi] * w_ref[0:1, lo:hi]
    out = out + scr[HALO - 1:HALO - 1 + t, lo:hi] * w_ref[1:2, lo:hi]
    return out + scr[HALO:HALO + t, lo:hi] * w_ref[2:3, lo:hi]


def _resident(block_shape, block_index=None):
    index = tuple(block_index) if block_index is not None else (0,) * len(block_shape)
    return pl.BlockSpec(block_shape, lambda *_: index, pipeline_mode=pl.Buffered(1))


def _bias_columns(c, head):
    lane = lax.broadcasted_iota(jnp.int32, c.shape, 1)
    pieces = _split3(c[:, head:head + 1])
    k_side = jnp.where((lane >= 3) & (lane < 6), 1.0, 0.0)
    q_side = jnp.where(lane < 3, 1.0, 0.0)
    for n, piece in enumerate(pieces):
        k_side = jnp.where(lane == n, -piece, k_side)
        q_side = jnp.where(lane == 3 + n, piece, q_side)
    return k_side.astype(BF16), q_side.astype(BF16)


def _mixer_in_kernel(x_ref, g_ref, wqkv_ref, wf_ref, bf_ref, wcv_ref, cw_ref, halo_ref, c0_ref,
                     qf_ref, kf_ref, vt_ref, yc_ref, *rest, tail_at):
    if tail_at is not None:
        c_ref, tail_ref, cu_scr, carry = rest
    else:
        cu_scr, carry = rest
    t = x_ref.shape[0]

    @pl.when(pl.program_id(1) == 0)
    def _():
        cu_scr[0:HALO, :] = halo_ref[...]
        carry[...] = c0_ref[...]

    h = _rms_norm(x_ref[...], g_ref[...]).astype(BF16)

    gc = _dot(h, wcv_ref[:, D_MODEL:2 * D_MODEL])
    u = _dot(h, wcv_ref[:, 2 * D_MODEL:3 * D_MODEL])
    cu_scr[HALO:HALO + t, :] = gc * u
    conv = _causal_conv(cu_scr, 0, D_MODEL, t, cw_ref)
    if tail_at is not None:
        tail_ref[...] = cu_scr[tail_at:tail_at + HALO, :]
    cu_scr[0:HALO, :] = cu_scr[t:t + HALO, :]

    f = _dot(h, wf_ref[...]) + bf_ref[...]
    log_f = (jnp.minimum(f, 0.0) - jnp.log1p(jnp.exp(-jnp.abs(f)))) * LOG2E
    c, carry[...] = _cumsum_rows(log_f, carry[...])
    if tail_at is not None:
        c_ref[...] = c
        c = jnp.where(lax.broadcasted_iota(jnp.int32, c.shape, 0) < tail_at, c, -MASKED)
    for head in range(N_HEADS):
        lo = head * 2 * HEAD_DIM + HEAD_DIM
        kf_ref[:, lo:lo + HEAD_DIM], qf_ref[:, lo:lo + HEAD_DIM] = _bias_columns(c, head)

    vt_ref[...] = _dot(h, wqkv_ref[:, 2 * D_MODEL:3 * D_MODEL]).T.astype(BF16)
    q = (_dot(h, wqkv_ref[:, 0:D_MODEL]) * Q_SCALE).astype(BF16)
    k = _dot(h, wqkv_ref[:, D_MODEL:2 * D_MODEL]).astype(BF16)
    for head in range(N_HEADS):
        src = slice(head * HEAD_DIM, (head + 1) * HEAD_DIM)
        dst = slice(head * 2 * HEAD_DIM, head * 2 * HEAD_DIM + HEAD_DIM)
        qf_ref[:, dst] = q[:, src]
        kf_ref[:, dst] = k[:, src]
    yc_ref[...] = (_dot(h, wcv_ref[:, 0:D_MODEL]) * conv).astype(BF16)


def _mixer_in(x, g_mix, w_in, b_f, conv_w, halo, c0, *, tile, tail_at=None):
    b, n, _ = x.shape
    rows = pl.BlockSpec((None, tile, D_MODEL), lambda bi, i: (bi, i, 0))
    wide = pl.BlockSpec((None, tile, 2 * D_MODEL), lambda bi, i: (bi, i, 0))
    out_shape = [jax.ShapeDtypeStruct((b, n, 2 * D_MODEL), BF16), jax.ShapeDtypeStruct((b, n, 2 * D_MODEL), BF16),
                 jax.ShapeDtypeStruct((b, D_MODEL, n), BF16), jax.ShapeDtypeStruct((b, n, D_MODEL), BF16)]
    out_specs = [wide, wide, pl.BlockSpec((None, D_MODEL, tile), lambda bi, i: (bi, 0, i)), rows]
    if tail_at is not None:
        assert b == 1 and n == tile
        out_shape += [jax.ShapeDtypeStruct((b, n, LANES), F32), jax.ShapeDtypeStruct((HALO, D_MODEL), F32)]
        out_specs += [pl.BlockSpec((None, tile, LANES), lambda bi, i: (bi, i, 0)),
                      pl.BlockSpec((HALO, D_MODEL), lambda bi, i: (0, 0))]
    return pl.pallas_call(
        functools.partial(_mixer_in_kernel, tail_at=tail_at),
        grid=(b, n // tile),
        in_specs=[rows, _resident((1, D_MODEL)), _resident(*W_IN_BLOCKS["qkv"]), _resident(*W_IN_BLOCKS["f"]),
                  _resident(b_f.shape), _resident(*W_IN_BLOCKS["conv"]), _resident(conv_w.shape),
                  _resident(halo.shape), _resident(c0.shape)],
        out_specs=out_specs,
        out_shape=out_shape,
        scratch_shapes=[pltpu.VMEM((tile + HALO, D_MODEL), F32), pltpu.VMEM((1, LANES), F32)],
        compiler_params=pltpu.CompilerParams(dimension_semantics=("arbitrary", "arbitrary"),
                                             vmem_limit_bytes=VMEM_LIMIT),
        name="mixer_in" if tail_at is None else "mixer_in_meta",
    )(x, g_mix, w_in, w_in, b_f, w_in, conv_w, halo, c0)


def _attention_kernel(qf_ref, *rest, has_prefix, multi):
    rest = list(rest)
    qfn_ref = rest.pop(0) if multi else None
    kf_ref, vt_ref = rest.pop(0), rest.pop(0)
    kpf_ref, vtp_ref = (rest.pop(0), rest.pop(0)) if has_prefix else (None, None)
    o_ref, diag_scr, ab_scr, abmax_scr, m_scr, acc_scr = rest
    tq = qf_ref.shape[0]
    n_q = kf_ref.shape[0] // tq
    group = range(vt_ref.shape[0] // HEAD_DIM)
    i = pl.program_id(2)

    def cols(g):
        return slice(g * HEAD_DIM, (g + 1) * HEAD_DIM)

    def wide(g):
        return slice(g * 2 * HEAD_DIM, (g + 1) * 2 * HEAD_DIM)

    def rows_at(block):
        return pl.ds(pl.multiple_of(block * tq, tq), tq)

    def ones_rows(keys):
        first_row = lax.broadcasted_iota(jnp.int32, (BF16_ROWS, keys), 0) == 0
        return jnp.where(first_row, 1.0, 0.0).astype(BF16)

    key = lax.broadcasted_iota(jnp.int32, (tq, tq), 0)
    qry = lax.broadcasted_iota(jnp.int32, (tq, tq), 1)
    ones_tq = ones_rows(tq)
    ones_diag = ones_rows(tq + (kpf_ref.shape[0] if has_prefix else 0))
    q_cur = [qf_ref[:, wide(g)] for g in group]

    def produce(dst, block, q, max_dst=None):
        for g in group:
            s = _dot_nt(kf_ref[rows_at(block), wide(g)], q[g])
            dst[g] = s
            if max_dst is not None:
                max_dst[g] = jnp.max(s, axis=0, keepdims=True)

    def update(g, s, vt_ones, s_max=None):
        m_old = m_scr[g]
        m_new = jnp.maximum(m_old, jnp.max(s, axis=0, keepdims=True) if s_max is None else s_max)
        alpha = jnp.exp2(m_old - m_new)
        p = jnp.exp2(s - m_new).astype(BF16)
        acc_scr[g] = alpha * acc_scr[g] + _dot(vt_ones, p)
        m_scr[g] = m_new

    def prefix_scores():
        return [_dot_nt(kpf_ref[:, wide(g)], q_cur[g]) for g in group] if has_prefix else None

    def consume(src, block, s_prefix=None, diagonal=False, max_src=None):
        for g in group:
            s, vt, ones = src[g], vt_ref[cols(g), rows_at(block)], ones_tq
            if diagonal:
                s, ones = jnp.where(key <= qry, s, MASKED), ones_diag
                if has_prefix:
                    s = jnp.concatenate([s, s_prefix[g]], axis=0)
                    vt = jnp.concatenate([vt, vtp_ref[cols(g), :]], axis=1)
            update(g, s, jnp.concatenate([vt, ones], axis=0), None if max_src is None else max_src[g])

    @pl.when(i == 0)
    def _():
        produce(diag_scr, 0, q_cur)

    m_scr[...] = jnp.full(m_scr.shape, MASKED, F32)
    acc_scr[...] = jnp.zeros(acc_scr.shape, F32)

    if not multi:
        consume(diag_scr, 0, prefix_scores(), diagonal=True)
    else:
        first = i == 0
        nxt = jnp.minimum(i + 1, n_q - 1)
        q_nxt = [qfn_ref[:, wide(g)] for g in group]

        s_prefix = prefix_scores()
        slot = jnp.where(first, 0, (i - 1) & 1)
        produce(ab_scr.at[slot], jnp.where(first, nxt, 0),
                [jnp.where(first, q_nxt[g], q_cur[g]) for g in group], abmax_scr.at[slot])
        consume(diag_scr, i, s_prefix, diagonal=True)

        @pl.when(first)
        def _():
            diag_scr[...] = ab_scr[0]

        def step(j, src, dst):
            produce(ab_scr.at[dst], j + 1, q_cur, abmax_scr.at[dst])
            consume(ab_scr.at[src], j, max_src=abmax_scr.at[src])

        @pl.when((i >= 2) & ((i & 1) == 0))
        def _():
            step(0, 1, 0)

        def pair(jj, carry):
            j = 1 - (i & 1) + 2 * jj
            step(j, 0, 1)
            step(j + 1, 1, 0)
            return carry

        lax.fori_loop(0, lax.shift_right_logical(jnp.maximum(i - 1, 0), 1), pair, 0)

        @pl.when(i >= 1)
        def _():
            produce(diag_scr, nxt, q_nxt)
            consume(ab_scr.at[0], i - 1, max_src=abmax_scr.at[0])

    for g in group:
        acc = acc_scr[g]
        inv_l = 1.0 / acc[HEAD_DIM:HEAD_DIM + 1]
        o_ref[:, cols(g)] = (acc[0:HEAD_DIM] * inv_l).T.astype(o_ref.dtype)


def _attention(qf, kf, vt, prefix, *, tq, heads):
    b, n, _ = qf.shape
    n_q = n // tq
    width = heads * HEAD_DIM
    q_spec = pl.BlockSpec((None, tq, 2 * width), lambda bi, h, i: (bi, i, h))
    in_specs, args = [q_spec], [qf]
    if n_q > 1:
        in_specs.append(pl.BlockSpec((None, tq, 2 * width), lambda bi, h, i: (bi, jnp.minimum(i + 1, n_q - 1), h)))
        args.append(qf)
    in_specs += [pl.BlockSpec((None, n, 2 * width), lambda bi, h, i: (bi, 0, h)),
                 pl.BlockSpec((None, width, n), lambda bi, h, i: (bi, h, 0))]
    args += [kf, vt]
    if prefix is not None:
        kpf, vtp = prefix
        n_p = kpf.shape[0]
        in_specs += [pl.BlockSpec((n_p, 2 * width), lambda bi, h, i: (0, h)),
                     pl.BlockSpec((width, n_p), lambda bi, h, i: (h, 0))]
        args += [kpf, vtp]
    scratch = [pltpu.VMEM((heads, tq, tq), F32), pltpu.VMEM((2, heads, tq, tq), F32),
               pltpu.VMEM((2, heads, 1, tq), F32), pltpu.VMEM((heads, 1, tq), F32),
               pltpu.VMEM((heads, HEAD_DIM + BF16_ROWS, tq), F32)]
    return pl.pallas_call(
        functools.partial(_attention_kernel, has_prefix=prefix is not None, multi=n_q > 1),
        grid=(b, N_HEADS // heads, n_q),
        in_specs=in_specs,
        out_specs=pl.BlockSpec((None, tq, width), lambda bi, h, i: (bi, i, h)),
        out_shape=jax.ShapeDtypeStruct((b, n, D_MODEL), BF16),
        scratch_shapes=scratch,
        compiler_params=pltpu.CompilerParams(dimension_semantics=("arbitrary",) * 3,
                                             vmem_limit_bytes=VMEM_LIMIT),
        name="attention" if prefix is not None else "attention_meta",
    )(*args)


def _mixer_out_kernel(x_ref, att_ref, yc_ref, gmix_ref, gffn_ref, gfin_ref, wg_ref, woc_ref, woa_ref,
                      wo_ref, wa_ref, wv_ref, wd_ref, fcw_ref, halo_ref, out_ref, a_scr, *, tail_at):
    t = x_ref.shape[0]

    @pl.when(pl.program_id(1) == 0)
    def _():
        a_scr[0:HALO, :] = halo_ref[...]

    x = x_ref[...]
    h = _rms_norm(x, gmix_ref[...]).astype(BF16)
    g_att = _sigmoid(_dot(h, wg_ref[:, 0:D_MODEL]))
    g_conv = _sigmoid(_dot(h, wg_ref[:, D_MODEL:2 * D_MODEL]))
    mix = g_att * _dot(att_ref[...], woa_ref[...]) + g_conv * _dot(yc_ref[...], woc_ref[...])
    z1 = x + _dot(mix.astype(BF16), wo_ref[...])
    h2 = _rms_norm(z1, gffn_ref[...]).astype(BF16)

    if tail_at is not None:
        a_scr[HALO:HALO + t, :] = _dot(h2, wa_ref[...])
        out_ref[...] = a_scr[tail_at:tail_at + HALO, :]
        return

    ffn = jnp.zeros((t, D_MODEL), F32)
    for lo, hi in FF_CHUNKS:
        a_scr[HALO:HALO + t, lo:hi] = _dot(h2, wa_ref[:, lo:hi])
        conv = _causal_conv(a_scr, lo, hi, t, fcw_ref)
        act = conv * _sigmoid(conv) * _dot(h2, wv_ref[:, lo:hi])
        ffn = ffn + _dot(act.astype(BF16), wd_ref[lo:hi, :])
    a_scr[0:HALO, :] = a_scr[t:t + HALO, :]
    out_ref[...] = _rms_norm(z1 + ffn, gfin_ref[...])


def _mixer_out(x, att, yc, g_mix, g_ffn, g_final, w_in, w_sq, w_ffn, w_d, ffn_conv_w, halo, *, tile, tail_at=None):
    b, n, _ = x.shape
    square = [_resident((None, D_MODEL, D_MODEL), (k, 0, 0)) for k in range(3)]
    halves = [_resident((D_MODEL, D_FF), (0, k)) for k in range(2)]
    rows = pl.BlockSpec((None, tile, D_MODEL), lambda bi, i: (bi, i, 0))
    if tail_at is not None:
        assert b == 1 and n == tile
        out_shape = jax.ShapeDtypeStruct((HALO, D_FF), F32)
        out_spec = pl.BlockSpec((HALO, D_FF), lambda bi, i: (0, 0))
    else:
        out_shape = jax.ShapeDtypeStruct(x.shape, F32)
        out_spec = rows
    consts = (g_mix, g_ffn, g_final, w_in, w_sq, w_sq, w_sq, w_ffn, w_ffn, w_d, ffn_conv_w, halo)
    const_specs = ([_resident(g_mix.shape)] * 3 + [_resident(*W_IN_BLOCKS["gates"])] + square + halves
                   + [_resident(w_d.shape), _resident(ffn_conv_w.shape), _resident(halo.shape)])
    return pl.pallas_call(
        functools.partial(_mixer_out_kernel, tail_at=tail_at),
        grid=(b, n // tile),
        in_specs=[rows, rows, rows] + const_specs,
        out_specs=out_spec,
        out_shape=out_shape,
        scratch_shapes=[pltpu.VMEM((tile + HALO, D_FF), F32)],
        compiler_params=pltpu.CompilerParams(dimension_semantics=("arbitrary", "arbitrary"),
                                             vmem_limit_bytes=VMEM_LIMIT),
        name="mixer_out" if tail_at is None else "mixer_out_meta",
    )(x, att, yc, *consts)


def kernel(x, meta_tokens, g_mix, w_in, b_f, conv_w, w_o_attn, w_o_conv, w_o, g_ffn, w_ffn_in, ffn_conv_w,
           w_ffn_out, g_final):
    assert w_in.shape[0] == 1, "one layer"
    d = D_MODEL
    w = w_in[0]
    w_in_packed = jnp.concatenate(
        [w[:, 0:3 * d], w[:, 3 * d + N_HEADS:], jnp.pad(w[:, 3 * d:3 * d + N_HEADS], ((0, 0), (0, LANES - N_HEADS)))],
        axis=1).astype(BF16)
    b_f_row = jnp.pad(b_f[0], (0, LANES - N_HEADS))[None, :]
    w_sq = jnp.stack([w_o_conv[0], w_o_attn[0], w_o[0]]).astype(BF16)
    w_ffn = w_ffn_in[0].astype(BF16)
    w_d = w_ffn_out[0].astype(BF16)
    g_mix_row, g_ffn_row, g_final_row = g_mix[0][None, :], g_ffn[0][None, :], g_final[None, :]
    in_consts = (g_mix_row, w_in_packed, b_f_row, conv_w[0])
    out_consts = (g_mix_row, g_ffn_row, g_final_row, w_in_packed, w_sq, w_ffn, w_d, ffn_conv_w[0])

    meta = jnp.pad(meta_tokens.astype(F32), ((0, META_TILE - N_META), (0, 0)))[None]
    qfm, kfm, vtm, ycm, cm, cu_tail = _mixer_in(
        meta, *in_consts, jnp.zeros((HALO, d), F32), jnp.zeros((1, LANES), F32), tile=META_TILE, tail_at=N_META)
    att_m = _attention(qfm, kfm, vtm, None, tq=META_TILE, heads=HEADS_PER_STEP)
    a_tail = _mixer_out(meta, att_m, ycm, *out_consts, jnp.zeros((HALO, D_FF), F32),
                        tile=META_TILE, tail_at=N_META)

    qf, kf, vt, yc = _mixer_in(x, *in_consts, cu_tail, cm[0, N_META - 1:N_META, :], tile=ROW_TILE_IN)
    att = _attention(qf, kf, vt, (kfm[0], vtm[0]), tq=Q_TILE, heads=HEADS_PER_STEP)
    return _mixer_out(x, att, yc, *out_consts, a_tail, tile=ROW_TILE_OUT)
```

```python
import functools

import jax
import jax.numpy as jnp
from jax import lax
from jax.experimental import pallas as pl
from jax.experimental.pallas import tpu as pltpu

D_MODEL = 1024
N_HEADS = 8
HEAD_DIM = 128
N_META = 16
D_FF = 2816
RMS_EPS = 1e-6

LANES = 128
SUBLANES = 8
BF16_ROWS = 2 * SUBLANES
HALO = SUBLANES
VMEM_LIMIT = 56 * 1024 * 1024

LOG2E = 1.4426950408889634
Q_SCALE = HEAD_DIM ** -0.5 * LOG2E
MASKED = -1e30

META_TILE = LANES
ROW_TILE_IN = 512
ROW_TILE_OUT = 512
Q_TILE = 512
HEADS_PER_STEP = 4
FF_CHUNKS = ((0, 1536), (1536, D_FF))
W_IN_BLOCKS = {
    "qkv": ((D_MODEL, 3 * D_MODEL), (0, 0)),
    "conv": ((D_MODEL, 3 * D_MODEL), (0, 1)),
    "gates": ((D_MODEL, 2 * D_MODEL), (0, 3)),
    "f": ((D_MODEL, LANES), (0, 8 * D_MODEL // LANES)),
}

BF16 = jnp.bfloat16
F32 = jnp.float32


def _dot(a, b):
    return jnp.dot(a, b, preferred_element_type=F32)


def _dot_nt(a, b):
    return lax.dot_general(a, b, (((1,), (1,)), ((), ())), preferred_element_type=F32)


def _rms_norm(x, g):
    return x * lax.rsqrt(jnp.mean(x * x, axis=-1, keepdims=True) + RMS_EPS) * g


def _sigmoid(x):
    return 1.0 / (1.0 + jnp.exp(-x))


def _split3(x):
    hi = x.astype(BF16).astype(F32)
    mid = (x - hi).astype(BF16).astype(F32)
    lo = (x - hi - mid).astype(BF16).astype(F32)
    return hi, mid, lo


def _cumsum_rows(x, start):
    t = x.shape[0]
    row = lax.broadcasted_iota(jnp.int32, (LANES, LANES), 0)
    col = lax.broadcasted_iota(jnp.int32, (LANES, LANES), 1)
    tri = jnp.where(col <= row, 1.0, 0.0).astype(BF16)
    blocks = []
    for r in range(0, t, LANES):
        hi, mid, lo = _split3(x[r:r + LANES])
        local = _dot(tri, hi.astype(BF16)) + _dot(tri, mid.astype(BF16)) + _dot(tri, lo.astype(BF16))
        blocks.append(local + start)
        start = blocks[-1][LANES - 1:LANES]
    return jnp.concatenate(blocks, axis=0), start


def _causal_conv(scr, lo, hi, t, w_ref):
    out = scr[HALO - 2:HALO - 2 + t, lo:hi] * w_ref[0:1, lo:hi]
    out = out + scr[HALO - 1:HALO - 1 + t, lo:hi] * w_ref[1:2, lo:hi]
    return out + scr[HALO:HALO + t, lo:hi] * w_ref[2:3, lo:hi]


def _resident(block_shape, block_index=None):
    index = tuple(block_index) if block_index is not None else (0,) * len(block_shape)
    return pl.BlockSpec(block_shape, lambda *_: index, pipeline_mode=pl.Buffered(1))


def _repack_w_in_kernel(w_ref, o_ref):
    d = D_MODEL
    o_ref[:, 0:3 * d] = w_ref[:, 0:3 * d].astype(BF16)
    o_ref[:, 3 * d:8 * d] = w_ref[:, 3 * d + N_HEADS:8 * d + N_HEADS].astype(BF16)
    lane = lax.broadcasted_iota(jnp.int32, (w_ref.shape[0], LANES), 1)
    o_ref[:, 8 * d:] = jnp.where(lane < N_HEADS, w_ref[:, 3 * d:3 * d + LANES], 0.0).astype(BF16)


def _repack_w_in(w):
    rows = 256
    return pl.pallas_call(
        _repack_w_in_kernel,
        grid=(D_MODEL // rows,),
        in_specs=[pl.BlockSpec((rows, w.shape[1]), lambda i: (i, 0))],
        out_specs=pl.BlockSpec((rows, 8 * D_MODEL + LANES), lambda i: (i, 0)),
        out_shape=jax.ShapeDtypeStruct((D_MODEL, 8 * D_MODEL + LANES), BF16),
        compiler_params=pltpu.CompilerParams(dimension_semantics=("arbitrary",), vmem_limit_bytes=VMEM_LIMIT),
        name="repack_w_in",
    )(w)


def _bias_columns(c, head):
    lane = lax.broadcasted_iota(jnp.int32, c.shape, 1)
    pieces = _split3(c[:, head:head + 1])
    k_side = jnp.where((lane >= 3) & (lane < 6), 1.0, 0.0)
    q_side = jnp.where(lane < 3, 1.0, 0.0)
    for n, piece in enumerate(pieces):
        k_side = jnp.where(lane == n, -piece, k_side)
        q_side = jnp.where(lane == 3 + n, piece, q_side)
    return k_side.astype(BF16), q_side.astype(BF16)


def _mixer_in_kernel(x_ref, g_ref, wqkv_ref, wf_ref, bf_ref, wcv_ref, cw_ref, halo_ref, c0_ref,
                     qf_ref, kf_ref, vt_ref, yc_ref, *rest, tail_at):
    if tail_at is not None:
        c_ref, tail_ref, cu_scr, carry = rest
    else:
        cu_scr, carry = rest
    t = x_ref.shape[0]

    @pl.when(pl.program_id(1) == 0)
    def _():
        cu_scr[0:HALO, :] = halo_ref[...]
        carry[...] = c0_ref[...]

    h = _rms_norm(x_ref[...], g_ref[...]).astype(BF16)

    gc = _dot(h, wcv_ref[:, D_MODEL:2 * D_MODEL])
    u = _dot(h, wcv_ref[:, 2 * D_MODEL:3 * D_MODEL])
    cu_scr[HALO:HALO + t, :] = gc * u
    conv = _causal_conv(cu_scr, 0, D_MODEL, t, cw_ref)
    if tail_at is not None:
        tail_ref[...] = cu_scr[tail_at:tail_at + HALO, :]
    cu_scr[0:HALO, :] = cu_scr[t:t + HALO, :]

    f = _dot(h, wf_ref[...]) + bf_ref[...]
    log_f = (jnp.minimum(f, 0.0) - jnp.log1p(jnp.exp(-jnp.abs(f)))) * LOG2E
    c, carry[...] = _cumsum_rows(log_f, carry[...])
    if tail_at is not None:
        c_ref[...] = c
        c = jnp.where(lax.broadcasted_iota(jnp.int32, c.shape, 0) < tail_at, c, -MASKED)
    for head in range(N_HEADS):
        lo = head * 2 * HEAD_DIM + HEAD_DIM
        kf_ref[:, lo:lo + HEAD_DIM], qf_ref[:, lo:lo + HEAD_DIM] = _bias_columns(c, head)

    vt_ref[...] = _dot(h, wqkv_ref[:, 2 * D_MODEL:3 * D_MODEL]).T.astype(BF16)
    q = (_dot(h, wqkv_ref[:, 0:D_MODEL]) * Q_SCALE).astype(BF16)
    k = _dot(h, wqkv_ref[:, D_MODEL:2 * D_MODEL]).astype(BF16)
    for head in range(N_HEADS):
        src = slice(head * HEAD_DIM, (head + 1) * HEAD_DIM)
        dst = slice(head * 2 * HEAD_DIM, head * 2 * HEAD_DIM + HEAD_DIM)
        qf_ref[:, dst] = q[:, src]
        kf_ref[:, dst] = k[:, src]
    yc_ref[...] = (_dot(h, wcv_ref[:, 0:D_MODEL]) * conv).astype(BF16)


def _mixer_in(x, g_mix, w_in, b_f, conv_w, halo, c0, *, tile, tail_at=None):
    b, n, _ = x.shape
    rows = pl.BlockSpec((None, tile, D_MODEL), lambda bi, i: (bi, i, 0))
    wide = pl.BlockSpec((None, tile, 2 * D_MODEL), lambda bi, i: (bi, i, 0))
    out_shape = [jax.ShapeDtypeStruct((b, n, 2 * D_MODEL), BF16), jax.ShapeDtypeStruct((b, n, 2 * D_MODEL), BF16),
                 jax.ShapeDtypeStruct((b, D_MODEL, n), BF16), jax.ShapeDtypeStruct((b, n, D_MODEL), BF16)]
    out_specs = [wide, wide, pl.BlockSpec((None, D_MODEL, tile), lambda bi, i: (bi, 0, i)), rows]
    if tail_at is not None:
        assert b == 1 and n == tile
        out_shape += [jax.ShapeDtypeStruct((b, n, LANES), F32), jax.ShapeDtypeStruct((HALO, D_MODEL), F32)]
        out_specs += [pl.BlockSpec((None, tile, LANES), lambda bi, i: (bi, i, 0)),
                      pl.BlockSpec((HALO, D_MODEL), lambda bi, i: (0, 0))]
    return pl.pallas_call(
        functools.partial(_mixer_in_kernel, tail_at=tail_at),
        grid=(b, n // tile),
        in_specs=[rows, _resident((1, D_MODEL)), _resident(*W_IN_BLOCKS["qkv"]), _resident(*W_IN_BLOCKS["f"]),
                  _resident(b_f.shape), _resident(*W_IN_BLOCKS["conv"]), _resident(conv_w.shape),
                  _resident(halo.shape), _resident(c0.shape)],
        out_specs=out_specs,
        out_shape=out_shape,
        scratch_shapes=[pltpu.VMEM((tile + HALO, D_MODEL), F32), pltpu.VMEM((1, LANES), F32)],
        compiler_params=pltpu.CompilerParams(dimension_semantics=("arbitrary", "arbitrary"),
                                             vmem_limit_bytes=VMEM_LIMIT),
        name="mixer_in" if tail_at is None else "mixer_in_meta",
    )(x, g_mix, w_in, w_in, b_f, w_in, conv_w, halo, c0)


def _attention_kernel(qf_ref, *rest, has_prefix, multi):
    rest = list(rest)
    qfn_ref = rest.pop(0) if multi else None
    kf_ref, vt_ref = rest.pop(0), rest.pop(0)
    kpf_ref, vtp_ref = (rest.pop(0), rest.pop(0)) if has_prefix else (None, None)
    o_ref, diag_scr, ab_scr, abmax_scr, m_scr, acc_scr = rest
    tq = qf_ref.shape[0]
    n_q = kf_ref.shape[0] // tq
    group = range(vt_ref.shape[0] // HEAD_DIM)
    i = pl.program_id(2)

    def cols(g):
        return slice(g * HEAD_DIM, (g + 1) * HEAD_DIM)

    def wide(g):
        return slice(g * 2 * HEAD_DIM, (g + 1) * 2 * HEAD_DIM)

    def rows_at(block):
        return pl.ds(pl.multiple_of(block * tq, tq), tq)

    def ones_rows(keys):
        first_row = lax.broadcasted_iota(jnp.int32, (BF16_ROWS, keys), 0) == 0
        return jnp.where(first_row, 1.0, 0.0).astype(BF16)

    key = lax.broadcasted_iota(jnp.int32, (tq, tq), 0)
    qry = lax.broadcasted_iota(jnp.int32, (tq, tq), 1)
    ones_tq = ones_rows(tq)
    q_cur = [qf_ref[:, wide(g)] for g in group]

    def produce(dst, block, q, max_dst=None):
        for g in group:
            s = _dot_nt(kf_ref[rows_at(block), wide(g)], q[g])
            dst[g] = s
            if max_dst is not None:
                max_dst[g] = jnp.max(s, axis=0, keepdims=True)

    def update(g, s, vt_ones, s_max):
        m_old = m_scr[g]
        m_new = jnp.maximum(m_old, s_max)
        alpha = jnp.exp2(m_old - m_new)
        p = jnp.exp2(s - m_new).astype(BF16)
        acc_scr[g] = alpha * acc_scr[g] + _dot(vt_ones, p)
        m_scr[g] = m_new

    def prefix_scores():
        return [_dot_nt(kpf_ref[:, wide(g)], q_cur[g]) for g in group] if has_prefix else None

    def consume(src, block, max_src):
        for g in group:
            vt_ones = jnp.concatenate([vt_ref[cols(g), rows_at(block)], ones_tq], axis=0)
            update(g, src[g], vt_ones, max_src[g])

    def start(s_prefix):
        for g in group:
            s = jnp.where(key <= qry, diag_scr[g], MASKED)
            vt = vt_ref[cols(g), rows_at(i)]
            if has_prefix:
                s = jnp.concatenate([s, s_prefix[g]], axis=0)
                vt = jnp.concatenate([vt, vtp_ref[cols(g), :]], axis=1)
            s_max = jnp.max(s, axis=0, keepdims=True)
            vt_ones = jnp.concatenate([vt, ones_rows(s.shape[0])], axis=0)
            acc_scr[g] = _dot(vt_ones, jnp.exp2(s - s_max).astype(BF16))
            m_scr[g] = s_max

    @pl.when(i == 0)
    def _():
        produce(diag_scr, 0, q_cur)

    if not multi:
        start(prefix_scores())
    else:
        first = i == 0
        nxt = jnp.minimum(i + 1, n_q - 1)
        q_nxt = [qfn_ref[:, wide(g)] for g in group]

        s_prefix = prefix_scores()
        slot = jnp.where(first, 0, (i - 1) & 1)
        produce(ab_scr.at[slot], jnp.where(first, nxt, 0),
                [jnp.where(first, q_nxt[g], q_cur[g]) for g in group], abmax_scr.at[slot])
        start(s_prefix)

        @pl.when(first)
        def _():
            diag_scr[...] = ab_scr[0]

        def step(j, src, dst):
            produce(ab_scr.at[dst], j + 1, q_cur, abmax_scr.at[dst])
            consume(ab_scr.at[src], j, abmax_scr.at[src])

        @pl.when((i >= 2) & ((i & 1) == 0))
        def _():
            step(0, 1, 0)

        def pair(jj, carry):
            j = 1 - (i & 1) + 2 * jj
            step(j, 0, 1)
            step(j + 1, 1, 0)
            return carry

        lax.fori_loop(0, lax.shift_right_logical(jnp.maximum(i - 1, 0), 1), pair, 0)

        @pl.when(i >= 1)
        def _():
            produce(diag_scr, nxt, q_nxt)
            consume(ab_scr.at[0], i - 1, abmax_scr.at[0])

    for g in group:
        acc = acc_scr[g]
        inv_l = 1.0 / acc[HEAD_DIM:HEAD_DIM + 1]
        o_ref[:, cols(g)] = (acc[0:HEAD_DIM] * inv_l).T.astype(o_ref.dtype)


def _attention(qf, kf, vt, prefix, *, tq, heads):
    b, n, _ = qf.shape
    n_q = n // tq
    width = heads * HEAD_DIM
    q_spec = pl.BlockSpec((None, tq, 2 * width), lambda bi, h, i: (bi, i, h))
    in_specs, args = [q_spec], [qf]
    if n_q > 1:
        in_specs.append(pl.BlockSpec((None, tq, 2 * width), lambda bi, h, i: (bi, jnp.minimum(i + 1, n_q - 1), h)))
        args.append(qf)
    in_specs += [pl.BlockSpec((None, n, 2 * width), lambda bi, h, i: (bi, 0, h)),
                 pl.BlockSpec((None, width, n), lambda bi, h, i: (bi, h, 0))]
    args += [kf, vt]
    if prefix is not None:
        kpf, vtp = prefix
        n_p = kpf.shape[0]
        in_specs += [pl.BlockSpec((n_p, 2 * width), lambda bi, h, i: (0, h)),
                     pl.BlockSpec((width, n_p), lambda bi, h, i: (h, 0))]
        args += [kpf, vtp]
    scratch = [pltpu.VMEM((heads, tq, tq), F32), pltpu.VMEM((2, heads, tq, tq), F32),
               pltpu.VMEM((2, heads, 1, tq), F32), pltpu.VMEM((heads, 1, tq), F32),
               pltpu.VMEM((heads, HEAD_DIM + BF16_ROWS, tq), F32)]
    return pl.pallas_call(
        functools.partial(_attention_kernel, has_prefix=prefix is not None, multi=n_q > 1),
        grid=(b, N_HEADS // heads, n_q),
        in_specs=in_specs,
        out_specs=pl.BlockSpec((None, tq, width), lambda bi, h, i: (bi, i, h)),
        out_shape=jax.ShapeDtypeStruct((b, n, D_MODEL), BF16),
        scratch_shapes=scratch,
        compiler_params=pltpu.CompilerParams(dimension_semantics=("arbitrary",) * 3,
                                             vmem_limit_bytes=VMEM_LIMIT),
        name="attention" if prefix is not None else "attention_meta",
    )(*args)


def _mixer_out_kernel(x_ref, att_ref, yc_ref, gmix_ref, gffn_ref, gfin_ref, wg_ref, woc_ref, woa_ref,
                      wo_ref, wa_ref, wv_ref, wd_ref, fcw_ref, halo_ref, out_ref, a_scr, *, tail_at):
    t = x_ref.shape[0]

    @pl.when(pl.program_id(1) == 0)
    def _():
        a_scr[0:HALO, :] = halo_ref[...]

    x = x_ref[...]
    h = _rms_norm(x, gmix_ref[...]).astype(BF16)
    g_att = _sigmoid(_dot(h, wg_ref[:, 0:D_MODEL]))
    g_conv = _sigmoid(_dot(h, wg_ref[:, D_MODEL:2 * D_MODEL]))
    mix = g_att * _dot(att_ref[...], woa_ref[...]) + g_conv * _dot(yc_ref[...], woc_ref[...])
    z1 = x + _dot(mix.astype(BF16), wo_ref[...])
    h2 = _rms_norm(z1, gffn_ref[...]).astype(BF16)

    if tail_at is not None:
        a_scr[HALO:HALO + t, :] = _dot(h2, wa_ref[...])
        out_ref[...] = a_scr[tail_at:tail_at + HALO, :]
        return

    ffn = jnp.zeros((t, D_MODEL), F32)
    for lo, hi in FF_CHUNKS:
        a_scr[HALO:HALO + t, lo:hi] = _dot(h2, wa_ref[:, lo:hi])
        conv = _causal_conv(a_scr, lo, hi, t, fcw_ref)
        act = conv * _sigmoid(conv) * _dot(h2, wv_ref[:, lo:hi])
        ffn = ffn + _dot(act.astype(BF16), wd_ref[lo:hi, :])
    a_scr[0:HALO, :] = a_scr[t:t + HALO, :]
    out_ref[...] = _rms_norm(z1 + ffn, gfin_ref[...])


def _mixer_out(x, att, yc, g_mix, g_ffn, g_final, w_in, w_sq, w_ffn, w_d, ffn_conv_w, halo, *, tile, tail_at=None):
    b, n, _ = x.shape
    square = [_resident((None, D_MODEL, D_MODEL), (k, 0, 0)) for k in range(3)]
    halves = [_resident((D_MODEL, D_FF), (0, k)) for k in range(2)]
    rows = pl.BlockSpec((None, tile, D_MODEL), lambda bi, i: (bi, i, 0))
    if tail_at is not None:
        assert b == 1 and n == tile
        out_shape = jax.ShapeDtypeStruct((HALO, D_FF), F32)
        out_spec = pl.BlockSpec((HALO, D_FF), lambda bi, i: (0, 0))
    else:
        out_shape = jax.ShapeDtypeStruct(x.shape, F32)
        out_spec = rows
    consts = (g_mix, g_ffn, g_final, w_in, w_sq, w_sq, w_sq, w_ffn, w_ffn, w_d, ffn_conv_w, halo)
    const_specs = ([_resident(g_mix.shape)] * 3 + [_resident(*W_IN_BLOCKS["gates"])] + square + halves
                   + [_resident(w_d.shape), _resident(ffn_conv_w.shape), _resident(halo.shape)])
    return pl.pallas_call(
        functools.partial(_mixer_out_kernel, tail_at=tail_at),
        grid=(b, n // tile),
        in_specs=[rows, rows, rows] + const_specs,
        out_specs=out_spec,
        out_shape=out_shape,
        scratch_shapes=[pltpu.VMEM((tile + HALO, D_FF), F32)],
        compiler_params=pltpu.CompilerParams(dimension_semantics=("arbitrary", "arbitrary"),
                                             vmem_limit_bytes=VMEM_LIMIT),
        name="mixer_out" if tail_at is None else "mixer_out_meta",
    )(x, att, yc, *consts)


def kernel(x, meta_tokens, g_mix, w_in, b_f, conv_w, w_o_attn, w_o_conv, w_o, g_ffn, w_ffn_in, ffn_conv_w,
           w_ffn_out, g_final):
    assert w_in.shape[0] == 1, "one layer"
    d = D_MODEL
    w_in_packed = _repack_w_in(w_in[0])
    b_f_row = jnp.pad(b_f[0], (0, LANES - N_HEADS))[None, :]
    w_sq = jnp.stack([w_o_conv[0], w_o_attn[0], w_o[0]]).astype(BF16)
    w_ffn = w_ffn_in[0].astype(BF16)
    w_d = w_ffn_out[0].astype(BF16)
    g_mix_row, g_ffn_row, g_final_row = g_mix[0][None, :], g_ffn[0][None, :], g_final[None, :]
    in_consts = (g_mix_row, w_in_packed, b_f_row, conv_w[0])
    out_consts = (g_mix_row, g_ffn_row, g_final_row, w_in_packed, w_sq, w_ffn, w_d, ffn_conv_w[0])

    meta = jnp.pad(meta_tokens.astype(F32), ((0, META_TILE - N_META), (0, 0)))[None]
    qfm, kfm, vtm, ycm, cm, cu_tail = _mixer_in(
        meta, *in_consts, jnp.zeros((HALO, d), F32), jnp.zeros((1, LANES), F32), tile=META_TILE, tail_at=N_META)
    att_m = _attention(qfm, kfm, vtm, None, tq=META_TILE, heads=HEADS_PER_STEP)
    a_tail = _mixer_out(meta, att_m, ycm, *out_consts, jnp.zeros((HALO, D_FF), F32),
                        tile=META_TILE, tail_at=N_META)

    qf, kf, vt, yc = _mixer_in(x, *in_consts, cu_tail, cm[0, N_META - 1:N_META, :], tile=ROW_TILE_IN)
    att = _attention(qf, kf, vt, (kfm[0], vtm[0]), tq=Q_TILE, heads=HEADS_PER_STEP)
    return _mixer_out(x, att, yc, *out_consts, a_tail, tile=ROW_TILE_OUT)
```

```python
import functools

import jax
import jax.numpy as jnp
from jax import lax
from jax.experimental import pallas as pl
from jax.experimental.pallas import tpu as pltpu

D_MODEL = 1024
N_HEADS = 8
HEAD_DIM = 128
N_META = 16
D_FF = 2816
RMS_EPS = 1e-6

LANES = 128
SUBLANES = 8
BF16_ROWS = 2 * SUBLANES
HALO = SUBLANES
VMEM_LIMIT = 56 * 1024 * 1024

LOG2E = 1.4426950408889634
Q_SCALE = HEAD_DIM ** -0.5 * LOG2E
MASKED = -1e30

META_TILE = LANES
ROW_TILE_IN = 512
ROW_TILE_OUT = 512
Q_TILE = 512
HEADS_PER_STEP = 4
FF_CHUNKS = ((0, 1536), (1536, D_FF))
W_IN_BLOCKS = {
    "qkv": ((D_MODEL, 3 * D_MODEL), (0, 0)),
    "conv": ((D_MODEL, 3 * D_MODEL), (0, 1)),
    "gates": ((D_MODEL, 2 * D_MODEL), (0, 3)),
}

BF16 = jnp.bfloat16
F32 = jnp.float32


def _dot(a, b):
    return jnp.dot(a, b, preferred_element_type=F32)


def _dot_nt(a, b):
    return lax.dot_general(a, b, (((1,), (1,)), ((), ())), preferred_element_type=F32)


def _rms_norm(x, g):
    return x * lax.rsqrt(jnp.mean(x * x, axis=-1, keepdims=True) + RMS_EPS) * g


def _sigmoid(x):
    return 1.0 / (1.0 + jnp.exp(-x))


def _split3(x):
    hi = x.astype(BF16).astype(F32)
    mid = (x - hi).astype(BF16).astype(F32)
    lo = (x - hi - mid).astype(BF16).astype(F32)
    return hi, mid, lo


def _cumsum_rows(x, start):
    t = x.shape[0]
    row = lax.broadcasted_iota(jnp.int32, (LANES, LANES), 0)
    col = lax.broadcasted_iota(jnp.int32, (LANES, LANES), 1)
    tri = jnp.where(col <= row, 1.0, 0.0).astype(BF16)
    blocks = []
    for r in range(0, t, LANES):
        hi, mid, lo = _split3(x[r:r + LANES])
        local = _dot(tri, hi.astype(BF16)) + _dot(tri, mid.astype(BF16)) + _dot(tri, lo.astype(BF16))
        blocks.append(local + start)
        start = blocks[-1][LANES - 1:LANES]
    return jnp.concatenate(blocks, axis=0), start


def _causal_conv(scr, lo, hi, t, w_ref):
    out = scr[HALO - 2:HALO - 2 + t, lo:hi] * w_ref[0:1, lo:hi]
    out = out + scr[HALO - 1:HALO - 1 + t, lo:hi] * w_ref[1:2, lo:hi]
    return out + scr[HALO:HALO + t, lo:hi] * w_ref[2:3, lo:hi]


def _resident(block_shape, block_index=None):
    index = tuple(block_index) if block_index is not None else (0,) * len(block_shape)
    return pl.BlockSpec(block_shape, lambda *_: index, pipeline_mode=pl.Buffered(1))


def _repack_w_in_kernel(w_ref, f_ref, o_ref, of_ref):
    o_ref[...] = w_ref[...].T.astype(BF16)
    f_rows = jnp.concatenate([f_ref[...], jnp.zeros((LANES - N_HEADS, D_MODEL), F32)], axis=0)
    of_ref[...] = f_rows.T.astype(BF16)


def _repack_w_in(w_t):
    d = D_MODEL
    assert N_HEADS % SUBLANES == 0
    return pl.pallas_call(
        _repack_w_in_kernel,
        grid=(8,),
        in_specs=[pl.BlockSpec((pl.Element(d), pl.Element(d)),
                               lambda j: ((j * (d // N_HEADS) + jnp.where(j >= 3, 1, 0)) * N_HEADS, 0)),
                  pl.BlockSpec((N_HEADS, d), lambda j: (3 * d // N_HEADS, 0))],
        out_specs=[pl.BlockSpec((d, d), lambda j: (0, j)), pl.BlockSpec((d, LANES), lambda j: (0, 0))],
        out_shape=[jax.ShapeDtypeStruct((d, 8 * d), BF16), jax.ShapeDtypeStruct((d, LANES), BF16)],
        compiler_params=pltpu.CompilerParams(dimension_semantics=("arbitrary",), vmem_limit_bytes=VMEM_LIMIT),
        name="repack_w_in",
    )(w_t, w_t)


def _bias_columns(c, head):
    lane = lax.broadcasted_iota(jnp.int32, c.shape, 1)
    pieces = _split3(c[:, head:head + 1])
    k_side = jnp.where((lane >= 3) & (lane < 6), 1.0, 0.0)
    q_side = jnp.where(lane < 3, 1.0, 0.0)
    for n, piece in enumerate(pieces):
        k_side = jnp.where(lane == n, -piece, k_side)
        q_side = jnp.where(lane == 3 + n, piece, q_side)
    return k_side.astype(BF16), q_side.astype(BF16)


def _mixer_in_kernel(x_ref, g_ref, wqkv_ref, wf_ref, bf_ref, wcv_ref, cw_ref, halo_ref, c0_ref,
                     qf_ref, kf_ref, vt_ref, yc_ref, *rest, tail_at):
    if tail_at is not None:
        c_ref, tail_ref, cu_scr, carry = rest
    else:
        cu_scr, carry = rest
    t = x_ref.shape[0]

    @pl.when(pl.program_id(1) == 0)
    def _():
        cu_scr[0:HALO, :] = halo_ref[...]
        carry[...] = c0_ref[...]

    h = _rms_norm(x_ref[...], g_ref[...]).astype(BF16)

    gc = _dot(h, wcv_ref[:, D_MODEL:2 * D_MODEL])
    u = _dot(h, wcv_ref[:, 2 * D_MODEL:3 * D_MODEL])
    cu_scr[HALO:HALO + t, :] = gc * u
    conv = _causal_conv(cu_scr, 0, D_MODEL, t, cw_ref)
    if tail_at is not None:
        tail_ref[...] = cu_scr[tail_at:tail_at + HALO, :]
    cu_scr[0:HALO, :] = cu_scr[t:t + HALO, :]

    f = _dot(h, wf_ref[...]) + bf_ref[...]
    log_f = (jnp.minimum(f, 0.0) - jnp.log1p(jnp.exp(-jnp.abs(f)))) * LOG2E
    c, carry[...] = _cumsum_rows(log_f, carry[...])
    if tail_at is not None:
        c_ref[...] = c
        c = jnp.where(lax.broadcasted_iota(jnp.int32, c.shape, 0) < tail_at, c, -MASKED)
    for head in range(N_HEADS):
        lo = head * 2 * HEAD_DIM + HEAD_DIM
        kf_ref[:, lo:lo + HEAD_DIM], qf_ref[:, lo:lo + HEAD_DIM] = _bias_columns(c, head)

    vt_ref[...] = _dot(h, wqkv_ref[:, 2 * D_MODEL:3 * D_MODEL]).T.astype(BF16)
    q = (_dot(h, wqkv_ref[:, 0:D_MODEL]) * Q_SCALE).astype(BF16)
    k = _dot(h, wqkv_ref[:, D_MODEL:2 * D_MODEL]).astype(BF16)
    for head in range(N_HEADS):
        src = slice(head * HEAD_DIM, (head + 1) * HEAD_DIM)
        dst = slice(head * 2 * HEAD_DIM, head * 2 * HEAD_DIM + HEAD_DIM)
        qf_ref[:, dst] = q[:, src]
        kf_ref[:, dst] = k[:, src]
    yc_ref[...] = (_dot(h, wcv_ref[:, 0:D_MODEL]) * conv).astype(BF16)


def _mixer_in(x, g_mix, w_in, w_f, b_f, conv_w, halo, c0, *, tile, tail_at=None):
    b, n, _ = x.shape
    rows = pl.BlockSpec((None, tile, D_MODEL), lambda bi, i: (bi, i, 0))
    wide = pl.BlockSpec((None, tile, 2 * D_MODEL), lambda bi, i: (bi, i, 0))
    out_shape = [jax.ShapeDtypeStruct((b, n, 2 * D_MODEL), BF16), jax.ShapeDtypeStruct((b, n, 2 * D_MODEL), BF16),
                 jax.ShapeDtypeStruct((b, D_MODEL, n), BF16), jax.ShapeDtypeStruct((b, n, D_MODEL), BF16)]
    out_specs = [wide, wide, pl.BlockSpec((None, D_MODEL, tile), lambda bi, i: (bi, 0, i)), rows]
    if tail_at is not None:
        assert b == 1 and n == tile
        out_shape += [jax.ShapeDtypeStruct((b, n, LANES), F32), jax.ShapeDtypeStruct((HALO, D_MODEL), F32)]
        out_specs += [pl.BlockSpec((None, tile, LANES), lambda bi, i: (bi, i, 0)),
                      pl.BlockSpec((HALO, D_MODEL), lambda bi, i: (0, 0))]
    return pl.pallas_call(
        functools.partial(_mixer_in_kernel, tail_at=tail_at),
        grid=(b, n // tile),
        in_specs=[rows, _resident((1, D_MODEL)), _resident(*W_IN_BLOCKS["qkv"]), _resident(w_f.shape),
                  _resident(b_f.shape), _resident(*W_IN_BLOCKS["conv"]), _resident(conv_w.shape),
                  _resident(halo.shape), _resident(c0.shape)],
        out_specs=out_specs,
        out_shape=out_shape,
        scratch_shapes=[pltpu.VMEM((tile + HALO, D_MODEL), F32), pltpu.VMEM((1, LANES), F32)],
        compiler_params=pltpu.CompilerParams(dimension_semantics=("arbitrary", "arbitrary"),
                                             vmem_limit_bytes=VMEM_LIMIT),
        name="mixer_in" if tail_at is None else "mixer_in_meta",
    )(x, g_mix, w_in, w_f, b_f, w_in, conv_w, halo, c0)


def _attention_kernel(qf_ref, *rest, has_prefix, multi):
    rest = list(rest)
    qfn_ref = rest.pop(0) if multi else None
    kf_ref, vt_ref = rest.pop(0), rest.pop(0)
    kpf_ref, vtp_ref = (rest.pop(0), rest.pop(0)) if has_prefix else (None, None)
    o_ref, diag_scr, ab_scr, abmax_scr, m_scr, acc_scr = rest
    tq = qf_ref.shape[0]
    n_q = kf_ref.shape[0] // tq
    group = range(vt_ref.shape[0] // HEAD_DIM)
    i = pl.program_id(2)

    def cols(g):
        return slice(g * HEAD_DIM, (g + 1) * HEAD_DIM)

    def wide(g):
        return slice(g * 2 * HEAD_DIM, (g + 1) * 2 * HEAD_DIM)

    def rows_at(block):
        return pl.ds(pl.multiple_of(block * tq, tq), tq)

    def ones_rows(keys):
        first_row = lax.broadcasted_iota(jnp.int32, (BF16_ROWS, keys), 0) == 0
        return jnp.where(first_row, 1.0, 0.0).astype(BF16)

    key = lax.broadcasted_iota(jnp.int32, (tq, tq), 0)
    qry = lax.broadcasted_iota(jnp.int32, (tq, tq), 1)
    ones_tq = ones_rows(tq)
    q_cur = [qf_ref[:, wide(g)] for g in group]

    def produce(dst, block, q, max_dst=None):
        for g in group:
            s = _dot_nt(kf_ref[rows_at(block), wide(g)], q[g])
            dst[g] = s
            if max_dst is not None:
                max_dst[g] = jnp.max(s, axis=0, keepdims=True)

    def update(g, s, vt_ones, s_max):
        m_old = m_scr[g]
        m_new = jnp.maximum(m_old, s_max)
        alpha = jnp.exp2(m_old - m_new)
        p = jnp.exp2(s - m_new).astype(BF16)
        acc_scr[g] = alpha * acc_scr[g] + _dot(vt_ones, p)
        m_scr[g] = m_new

    def prefix_scores():
        return [_dot_nt(kpf_ref[:, wide(g)], q_cur[g]) for g in group] if has_prefix else None

    def consume(src, block, max_src):
        for g in group:
            vt_ones = jnp.concatenate([vt_ref[cols(g), rows_at(block)], ones_tq], axis=0)
            update(g, src[g], vt_ones, max_src[g])

    def start(s_prefix):
        for g in group:
            s = jnp.where(key <= qry, diag_scr[g], MASKED)
            vt = vt_ref[cols(g), rows_at(i)]
            if has_prefix:
                s = jnp.concatenate([s, s_prefix[g]], axis=0)
                vt = jnp.concatenate([vt, vtp_ref[cols(g), :]], axis=1)
            s_max = jnp.max(s, axis=0, keepdims=True)
            vt_ones = jnp.concatenate([vt, ones_rows(s.shape[0])], axis=0)
            acc_scr[g] = _dot(vt_ones, jnp.exp2(s - s_max).astype(BF16))
            m_scr[g] = s_max

    @pl.when(i == 0)
    def _():
        produce(diag_scr, 0, q_cur)

    if not multi:
        start(prefix_scores())
    else:
        first = i == 0
        nxt = jnp.minimum(i + 1, n_q - 1)
        q_nxt = [qfn_ref[:, wide(g)] for g in group]

        s_prefix = prefix_scores()
        slot = jnp.where(first, 0, (i - 1) & 1)
        produce(ab_scr.at[slot], jnp.where(first, nxt, 0),
                [jnp.where(first, q_nxt[g], q_cur[g]) for g in group], abmax_scr.at[slot])
        start(s_prefix)

        @pl.when(first)
        def _():
            diag_scr[...] = ab_scr[0]

        def step(j, src, dst):
            produce(ab_scr.at[dst], j + 1, q_cur, abmax_scr.at[dst])
            consume(ab_scr.at[src], j, abmax_scr.at[src])

        @pl.when((i >= 2) & ((i & 1) == 0))
        def _():
            step(0, 1, 0)

        def pair(jj, carry):
            j = 1 - (i & 1) + 2 * jj
            step(j, 0, 1)
            step(j + 1, 1, 0)
            return carry

        lax.fori_loop(0, lax.shift_right_logical(jnp.maximum(i - 1, 0), 1), pair, 0)

        @pl.when(i >= 1)
        def _():
            produce(diag_scr, nxt, q_nxt)
            consume(ab_scr.at[0], i - 1, abmax_scr.at[0])

    for g in group:
        acc = acc_scr[g]
        inv_l = 1.0 / acc[HEAD_DIM:HEAD_DIM + 1]
        o_ref[:, cols(g)] = (acc[0:HEAD_DIM] * inv_l).T.astype(o_ref.dtype)


def _attention(qf, kf, vt, prefix, *, tq, heads):
    b, n, _ = qf.shape
    n_q = n // tq
    width = heads * HEAD_DIM
    q_spec = pl.BlockSpec((None, tq, 2 * width), lambda bi, h, i: (bi, i, h))
    in_specs, args = [q_spec], [qf]
    if n_q > 1:
        in_specs.append(pl.BlockSpec((None, tq, 2 * width), lambda bi, h, i: (bi, jnp.minimum(i + 1, n_q - 1), h)))
        args.append(qf)
    in_specs += [pl.BlockSpec((None, n, 2 * width), lambda bi, h, i: (bi, 0, h)),
                 pl.BlockSpec((None, width, n), lambda bi, h, i: (bi, h, 0))]
    args += [kf, vt]
    if prefix is not None:
        kpf, vtp = prefix
        n_p = kpf.shape[0]
        in_specs += [pl.BlockSpec((n_p, 2 * width), lambda bi, h, i: (0, h)),
                     pl.BlockSpec((width, n_p), lambda bi, h, i: (h, 0))]
        args += [kpf, vtp]
    scratch = [pltpu.VMEM((heads, tq, tq), F32), pltpu.VMEM((2, heads, tq, tq), F32),
               pltpu.VMEM((2, heads, 1, tq), F32), pltpu.VMEM((heads, 1, tq), F32),
               pltpu.VMEM((heads, HEAD_DIM + BF16_ROWS, tq), F32)]
    return pl.pallas_call(
        functools.partial(_attention_kernel, has_prefix=prefix is not None, multi=n_q > 1),
        grid=(b, N_HEADS // heads, n_q),
        in_specs=in_specs,
        out_specs=pl.BlockSpec((None, tq, width), lambda bi, h, i: (bi, i, h)),
        out_shape=jax.ShapeDtypeStruct((b, n, D_MODEL), BF16),
        scratch_shapes=scratch,
        compiler_params=pltpu.CompilerParams(dimension_semantics=("arbitrary",) * 3,
                                             vmem_limit_bytes=VMEM_LIMIT),
        name="attention" if prefix is not None else "attention_meta",
    )(*args)


def _mixer_out_kernel(x_ref, att_ref, yc_ref, gmix_ref, gffn_ref, gfin_ref, wg_ref, woc_ref, woa_ref,
                      wo_ref, wa_ref, wv_ref, wd_ref, fcw_ref, halo_ref, out_ref, a_scr, *, tail_at):
    t = x_ref.shape[0]

    @pl.when(pl.program_id(1) == 0)
    def _():
        a_scr[0:HALO, :] = halo_ref[...]

    x = x_ref[...]
    h = _rms_norm(x, gmix_ref[...]).astype(BF16)
    g_att = _sigmoid(_dot(h, wg_ref[:, 0:D_MODEL]))
    g_conv = _sigmoid(_dot(h, wg_ref[:, D_MODEL:2 * D_MODEL]))
    mix = g_att * _dot(att_ref[...], woa_ref[...]) + g_conv * _dot(yc_ref[...], woc_ref[...])
    z1 = x + _dot(mix.astype(BF16), wo_ref[...])
    h2 = _rms_norm(z1, gffn_ref[...]).astype(BF16)

    if tail_at is not None:
        a_scr[HALO:HALO + t, :] = _dot(h2, wa_ref[...])
        out_ref[...] = a_scr[tail_at:tail_at + HALO, :]
        return

    ffn = jnp.zeros((t, D_MODEL), F32)
    for lo, hi in FF_CHUNKS:
        a_scr[HALO:HALO + t, lo:hi] = _dot(h2, wa_ref[:, lo:hi])
        conv = _causal_conv(a_scr, lo, hi, t, fcw_ref)
        act = conv * _sigmoid(conv) * _dot(h2, wv_ref[:, lo:hi])
        ffn = ffn + _dot(act.astype(BF16), wd_ref[lo:hi, :])
    a_scr[0:HALO, :] = a_scr[t:t + HALO, :]
    out_ref[...] = _rms_norm(z1 + ffn, gfin_ref[...])


def _mixer_out(x, att, yc, g_mix, g_ffn, g_final, w_in, w_sq, w_ffn, w_d, ffn_conv_w, halo, *, tile, tail_at=None):
    b, n, _ = x.shape
    square = [_resident((None, D_MODEL, D_MODEL), (k, 0, 0)) for k in range(3)]
    halves = [_resident((D_MODEL, D_FF), (0, k)) for k in range(2)]
    rows = pl.BlockSpec((None, tile, D_MODEL), lambda bi, i: (bi, i, 0))
    if tail_at is not None:
        assert b == 1 and n == tile
        out_shape = jax.ShapeDtypeStruct((HALO, D_FF), F32)
        out_spec = pl.BlockSpec((HALO, D_FF), lambda bi, i: (0, 0))
    else:
        out_shape = jax.ShapeDtypeStruct(x.shape, F32)
        out_spec = rows
    consts = (g_mix, g_ffn, g_final, w_in, w_sq, w_sq, w_sq, w_ffn, w_ffn, w_d, ffn_conv_w, halo)
    const_specs = ([_resident(g_mix.shape)] * 3 + [_resident(*W_IN_BLOCKS["gates"])] + square + halves
                   + [_resident(w_d.shape), _resident(ffn_conv_w.shape), _resident(halo.shape)])
    return pl.pallas_call(
        functools.partial(_mixer_out_kernel, tail_at=tail_at),
        grid=(b, n // tile),
        in_specs=[rows, rows, rows] + const_specs,
        out_specs=out_spec,
        out_shape=out_shape,
        scratch_shapes=[pltpu.VMEM((tile + HALO, D_FF), F32)],
        compiler_params=pltpu.CompilerParams(dimension_semantics=("arbitrary", "arbitrary"),
                                             vmem_limit_bytes=VMEM_LIMIT),
        name="mixer_out" if tail_at is None else "mixer_out_meta",
    )(x, att, yc, *consts)


def kernel(x, meta_tokens, g_mix, w_in, b_f, conv_w, w_o_attn, w_o_conv, w_o, g_ffn, w_ffn_in, ffn_conv_w,
           w_ffn_out, g_final):
    assert w_in.shape[0] == 1, "one layer"
    d = D_MODEL
    w_in_packed, w_f = _repack_w_in(w_in[0].T)
    b_f_row = jnp.pad(b_f[0], (0, LANES - N_HEADS))[None, :]
    w_sq = jnp.stack([w_o_conv[0], w_o_attn[0], w_o[0]]).astype(BF16)
    w_ffn = w_ffn_in[0].astype(BF16)
    w_d = w_ffn_out[0].astype(BF16)
    g_mix_row, g_ffn_row, g_final_row = g_mix[0][None, :], g_ffn[0][None, :], g_final[None, :]
    in_consts = (g_mix_row, w_in_packed, w_f, b_f_row, conv_w[0])
    out_consts = (g_mix_row, g_ffn_row, g_final_row, w_in_packed, w_sq, w_ffn, w_d, ffn_conv_w[0])

    meta = jnp.pad(meta_tokens.astype(F32), ((0, META_TILE - N_META), (0, 0)))[None]
    qfm, kfm, vtm, ycm, cm, cu_tail = _mixer_in(
        meta, *in_consts, jnp.zeros((HALO, d), F32), jnp.zeros((1, LANES), F32), tile=META_TILE, tail_at=N_META)
    att_m = _attention(qfm, kfm, vtm, None, tq=META_TILE, heads=HEADS_PER_STEP)
    a_tail = _mixer_out(meta, att_m, ycm, *out_consts, jnp.zeros((HALO, D_FF), F32),
                        tile=META_TILE, tail_at=N_META)

    qf, kf, vt, yc = _mixer_in(x, *in_consts, cu_tail, cm[0, N_META - 1:N_META, :], tile=ROW_TILE_IN)
    att = _attention(qf, kf, vt, (kfm[0], vtm[0]), tq=Q_TILE, heads=HEADS_PER_STEP)
    return _mixer_out(x, att, yc, *out_consts, a_tail, tile=ROW_TILE_OUT)
```

```python
import functools

import jax
import jax.numpy as jnp
from jax import lax
from jax.experimental import pallas as pl
from jax.experimental.pallas import tpu as pltpu

D_MODEL = 1024
N_HEADS = 8
HEAD_DIM = 128
N_META = 16
D_FF = 2816
RMS_EPS = 1e-6

LANES = 128
SUBLANES = 8
BF16_ROWS = 2 * SUBLANES
HALO = SUBLANES
VMEM_LIMIT = 56 * 1024 * 1024

LOG2E = 1.4426950408889634
Q_SCALE = HEAD_DIM ** -0.5 * LOG2E
MASKED = -1e30

META_TILE = LANES
ROW_TILE_IN = 512
ROW_TILE_OUT = 512
Q_TILE = 512
HEADS_PER_STEP = 4
W_IN_BLOCKS = {
    "qkv": ((D_MODEL, 3 * D_MODEL), (0, 0)),
    "conv": ((D_MODEL, 3 * D_MODEL), (0, 1)),
    "gates": ((D_MODEL, 2 * D_MODEL), (0, 3)),
}

BF16 = jnp.bfloat16
F32 = jnp.float32


def _dot(a, b):
    return jnp.dot(a, b, preferred_element_type=F32)


def _rms_norm(x, g):
    return x * lax.rsqrt(jnp.mean(x * x, axis=-1, keepdims=True) + RMS_EPS) * g


def _sigmoid(x):
    return 1.0 / (1.0 + jnp.exp(-x))


def _split3(x):
    hi = x.astype(BF16).astype(F32)
    mid = (x - hi).astype(BF16).astype(F32)
    lo = (x - hi - mid).astype(BF16).astype(F32)
    return hi, mid, lo


def _cumsum_rows(x, start):
    t = x.shape[0]
    row = lax.broadcasted_iota(jnp.int32, (LANES, LANES), 0)
    col = lax.broadcasted_iota(jnp.int32, (LANES, LANES), 1)
    tri = jnp.where(col <= row, 1.0, 0.0).astype(BF16)
    blocks = []
    for r in range(0, t, LANES):
        hi, mid, lo = _split3(x[r:r + LANES])
        local = _dot(tri, hi.astype(BF16)) + _dot(tri, mid.astype(BF16)) + _dot(tri, lo.astype(BF16))
        blocks.append(local + start)
        start = blocks[-1][LANES - 1:LANES]
    return jnp.concatenate(blocks, axis=0), start


def _causal_conv(scr, t, w_ref):
    out = scr[HALO - 2:HALO - 2 + t, :] * w_ref[0:1, :]
    out = out + scr[HALO - 1:HALO - 1 + t, :] * w_ref[1:2, :]
    return out + scr[HALO:HALO + t, :] * w_ref[2:3, :]


def _resident(block_shape, block_index=None):
    index = tuple(block_index) if block_index is not None else (0,) * len(block_shape)
    return pl.BlockSpec(block_shape, lambda *_: index, pipeline_mode=pl.Buffered(1))


def _repack_w_in_kernel(w_ref, f_ref, o_ref, of_ref):
    o_ref[...] = w_ref[...].T.astype(BF16)
    f_rows = jnp.concatenate([f_ref[...], jnp.zeros((LANES - N_HEADS, D_MODEL), F32)], axis=0)
    of_ref[...] = f_rows.T.astype(BF16)


def _repack_w_in(w_t):
    d = D_MODEL
    assert N_HEADS % SUBLANES == 0
    return pl.pallas_call(
        _repack_w_in_kernel,
        grid=(8,),
        in_specs=[pl.BlockSpec((pl.Element(d), pl.Element(d)),
                               lambda j: ((j * (d // N_HEADS) + jnp.where(j >= 3, 1, 0)) * N_HEADS, 0)),
                  pl.BlockSpec((N_HEADS, d), lambda j: (3 * d // N_HEADS, 0))],
        out_specs=[pl.BlockSpec((d, d), lambda j: (0, j)), pl.BlockSpec((d, LANES), lambda j: (0, 0))],
        out_shape=[jax.ShapeDtypeStruct((d, 8 * d), BF16), jax.ShapeDtypeStruct((d, LANES), BF16)],
        compiler_params=pltpu.CompilerParams(dimension_semantics=("arbitrary",), vmem_limit_bytes=VMEM_LIMIT),
        name="repack_w_in",
    )(w_t, w_t)


def _bias_columns(c, head):
    lane = lax.broadcasted_iota(jnp.int32, c.shape, 1)
    pieces = _split3(c[:, head:head + 1])
    k_side = jnp.where((lane >= 3) & (lane < 6), 1.0, 0.0)
    q_side = jnp.where(lane < 3, 1.0, 0.0)
    for n, piece in enumerate(pieces):
        k_side = jnp.where(lane == n, -piece, k_side)
        q_side = jnp.where(lane == 3 + n, piece, q_side)
    return k_side, q_side


def _mixer_in_kernel(x_ref, g_ref, wqkv_ref, wf_ref, bf_ref, wcv_ref, cw_ref, halo_ref, c0_ref,
                     qft_ref, kf_ref, vt_ref, yc_ref, *rest, tail_at):
    if tail_at is not None:
        c_ref, tail_ref, cu_scr, carry = rest
    else:
        cu_scr, carry = rest
    t = x_ref.shape[0]

    @pl.when(pl.program_id(1) == 0)
    def _():
        cu_scr[0:HALO, :] = halo_ref[...]
        carry[...] = c0_ref[...]

    h = _rms_norm(x_ref[...], g_ref[...]).astype(BF16)

    gc = _dot(h, wcv_ref[:, D_MODEL:2 * D_MODEL])
    u = _dot(h, wcv_ref[:, 2 * D_MODEL:3 * D_MODEL])
    cu_scr[HALO:HALO + t, :] = gc * u
    conv = _causal_conv(cu_scr, t, cw_ref)
    if tail_at is not None:
        tail_ref[...] = cu_scr[tail_at:tail_at + HALO, :]
    cu_scr[0:HALO, :] = cu_scr[t:t + HALO, :]

    f = _dot(h, wf_ref[...]) + bf_ref[...]
    log_f = (jnp.minimum(f, 0.0) - jnp.log1p(jnp.exp(-jnp.abs(f)))) * LOG2E
    c, carry[...] = _cumsum_rows(log_f, carry[...])
    if tail_at is not None:
        c_ref[...] = c
        c = jnp.where(lax.broadcasted_iota(jnp.int32, c.shape, 0) < tail_at, c, -MASKED)
    for head in range(N_HEADS):
        lo = head * 2 * HEAD_DIM + HEAD_DIM
        k_cols, q_cols = _bias_columns(c, head)
        kf_ref[:, lo:lo + HEAD_DIM] = k_cols.astype(BF16)
        qft_ref[lo:lo + HEAD_DIM, :] = q_cols.T.astype(BF16)

    vt_ref[...] = _dot(h, wqkv_ref[:, 2 * D_MODEL:3 * D_MODEL]).T.astype(BF16)
    q = _dot(h, wqkv_ref[:, 0:D_MODEL]) * Q_SCALE
    k = _dot(h, wqkv_ref[:, D_MODEL:2 * D_MODEL]).astype(BF16)
    for head in range(N_HEADS):
        src = slice(head * HEAD_DIM, (head + 1) * HEAD_DIM)
        dst = slice(head * 2 * HEAD_DIM, head * 2 * HEAD_DIM + HEAD_DIM)
        qft_ref[dst, :] = q[:, src].T.astype(BF16)
        kf_ref[:, dst] = k[:, src]
    yc_ref[...] = (_dot(h, wcv_ref[:, 0:D_MODEL]) * conv).astype(BF16)


def _mixer_in(x, g_mix, w_in, w_f, b_f, conv_w, halo, c0, *, tile, tail_at=None):
    b, n, _ = x.shape
    rows = pl.BlockSpec((None, tile, D_MODEL), lambda bi, i: (bi, i, 0))
    out_shape = [jax.ShapeDtypeStruct((b, 2 * D_MODEL, n), BF16), jax.ShapeDtypeStruct((b, n, 2 * D_MODEL), BF16),
                 jax.ShapeDtypeStruct((b, D_MODEL, n), BF16), jax.ShapeDtypeStruct((b, n, D_MODEL), BF16)]
    out_specs = [pl.BlockSpec((None, 2 * D_MODEL, tile), lambda bi, i: (bi, 0, i)),
                 pl.BlockSpec((None, tile, 2 * D_MODEL), lambda bi, i: (bi, i, 0)),
                 pl.BlockSpec((None, D_MODEL, tile), lambda bi, i: (bi, 0, i)), rows]
    if tail_at is not None:
        assert b == 1 and n == tile
        out_shape += [jax.ShapeDtypeStruct((b, n, LANES), F32), jax.ShapeDtypeStruct((HALO, D_MODEL), F32)]
        out_specs += [pl.BlockSpec((None, tile, LANES), lambda bi, i: (bi, i, 0)),
                      pl.BlockSpec((HALO, D_MODEL), lambda bi, i: (0, 0))]
    return pl.pallas_call(
        functools.partial(_mixer_in_kernel, tail_at=tail_at),
        grid=(b, n // tile),
        in_specs=[rows, _resident((1, D_MODEL)), _resident(*W_IN_BLOCKS["qkv"]), _resident(w_f.shape),
                  _resident(b_f.shape), _resident(*W_IN_BLOCKS["conv"]), _resident(conv_w.shape),
                  _resident(halo.shape), _resident(c0.shape)],
        out_specs=out_specs,
        out_shape=out_shape,
        scratch_shapes=[pltpu.VMEM((tile + HALO, D_MODEL), F32), pltpu.VMEM((1, LANES), F32)],
        compiler_params=pltpu.CompilerParams(dimension_semantics=("arbitrary", "arbitrary"),
                                             vmem_limit_bytes=VMEM_LIMIT),
        name="mixer_in" if tail_at is None else "mixer_in_meta",
    )(x, g_mix, w_in, w_f, b_f, w_in, conv_w, halo, c0)


def _attention_kernel(qft_ref, *rest, has_prefix, multi):
    rest = list(rest)
    qftn_ref = rest.pop(0) if multi else None
    kf_ref, vt_ref = rest.pop(0), rest.pop(0)
    kpf_ref, vtp_ref = (rest.pop(0), rest.pop(0)) if has_prefix else (None, None)
    o_ref, diag_scr, ab_scr, abmax_scr, m_scr, acc_scr = rest
    tq = qft_ref.shape[1]
    n_q = kf_ref.shape[0] // tq
    group = range(vt_ref.shape[0] // HEAD_DIM)
    i = pl.program_id(2)

    def cols(g):
        return slice(g * HEAD_DIM, (g + 1) * HEAD_DIM)

    def wide(g):
        return slice(g * 2 * HEAD_DIM, (g + 1) * 2 * HEAD_DIM)

    def rows_at(block):
        return pl.ds(pl.multiple_of(block * tq, tq), tq)

    def ones_rows(keys):
        first_row = lax.broadcasted_iota(jnp.int32, (BF16_ROWS, keys), 0) == 0
        return jnp.where(first_row, 1.0, 0.0).astype(BF16)

    key = lax.broadcasted_iota(jnp.int32, (tq, tq), 0)
    qry = lax.broadcasted_iota(jnp.int32, (tq, tq), 1)
    ones_tq = ones_rows(tq)
    q_cur = [qft_ref[wide(g), :] for g in group]

    def produce(dst, block, q, max_dst=None):
        for g in group:
            s = _dot(kf_ref[rows_at(block), wide(g)], q[g])
            dst[g] = s
            if max_dst is not None:
                max_dst[g] = jnp.max(s, axis=0, keepdims=True)

    def update(g, s, vt_ones, s_max):
        m_old = m_scr[g]
        m_new = jnp.maximum(m_old, s_max)
        alpha = jnp.exp2(m_old - m_new)
        p = jnp.exp2(s - m_new).astype(BF16)
        acc_scr[g] = alpha * acc_scr[g] + _dot(vt_ones, p)
        m_scr[g] = m_new

    def prefix_scores():
        return [_dot(kpf_ref[:, wide(g)], q_cur[g]) for g in group] if has_prefix else None

    def consume(src, block, max_src):
        for g in group:
            vt_ones = jnp.concatenate([vt_ref[cols(g), rows_at(block)], ones_tq], axis=0)
            update(g, src[g], vt_ones, max_src[g])

    def start(s_prefix):
        for g in group:
            s = jnp.where(key <= qry, diag_scr[g], MASKED)
            vt = vt_ref[cols(g), rows_at(i)]
            if has_prefix:
                s = jnp.concatenate([s, s_prefix[g]], axis=0)
                vt = jnp.concatenate([vt, vtp_ref[cols(g), :]], axis=1)
            s_max = jnp.max(s, axis=0, keepdims=True)
            vt_ones = jnp.concatenate([vt, ones_rows(s.shape[0])], axis=0)
            acc_scr[g] = _dot(vt_ones, jnp.exp2(s - s_max).astype(BF16))
            m_scr[g] = s_max

    @pl.when(i == 0)
    def _():
        produce(diag_scr, 0, q_cur)

    if not multi:
        start(prefix_scores())
    else:
        first = i == 0
        nxt = jnp.minimum(i + 1, n_q - 1)
        q_nxt = [qftn_ref[wide(g), :] for g in group]

        s_prefix = prefix_scores()
        slot = jnp.where(first, 0, (i - 1) & 1)
        produce(ab_scr.at[slot], jnp.where(first, nxt, 0),
                [jnp.where(first, q_nxt[g], q_cur[g]) for g in group], abmax_scr.at[slot])
        start(s_prefix)

        @pl.when(first)
        def _():
            diag_scr[...] = ab_scr[0]

        def step(j, src, dst):
            produce(ab_scr.at[dst], j + 1, q_cur, abmax_scr.at[dst])
            consume(ab_scr.at[src], j, abmax_scr.at[src])

        @pl.when((i >= 2) & ((i & 1) == 0))
        def _():
            step(0, 1, 0)

        def pair(jj, carry):
            j = 1 - (i & 1) + 2 * jj
            step(j, 0, 1)
            step(j + 1, 1, 0)
            return carry

        lax.fori_loop(0, lax.shift_right_logical(jnp.maximum(i - 1, 0), 1), pair, 0)

        @pl.when(i >= 1)
        def _():
            produce(diag_scr, nxt, q_nxt)
            consume(ab_scr.at[0], i - 1, abmax_scr.at[0])

    for g in group:
        acc = acc_scr[g]
        inv_l = 1.0 / acc[HEAD_DIM:HEAD_DIM + 1]
        o_ref[:, cols(g)] = (acc[0:HEAD_DIM] * inv_l).T.astype(o_ref.dtype)


def _attention(qft, kf, vt, prefix, *, tq, heads):
    b, n, _ = kf.shape
    n_q = n // tq
    width = heads * HEAD_DIM
    in_specs, args = [pl.BlockSpec((None, 2 * width, tq), lambda bi, h, i: (bi, h, i))], [qft]
    if n_q > 1:
        in_specs.append(pl.BlockSpec((None, 2 * width, tq), lambda bi, h, i: (bi, h, jnp.minimum(i + 1, n_q - 1))))
        args.append(qft)
    in_specs += [pl.BlockSpec((None, n, 2 * width), lambda bi, h, i: (bi, 0, h)),
                 pl.BlockSpec((None, width, n), lambda bi, h, i: (bi, h, 0))]
    args += [kf, vt]
    if prefix is not None:
        kpf, vtp = prefix
        n_p = kpf.shape[0]
        in_specs += [pl.BlockSpec((n_p, 2 * width), lambda bi, h, i: (0, h)),
                     pl.BlockSpec((width, n_p), lambda bi, h, i: (h, 0))]
        args += [kpf, vtp]
    scratch = [pltpu.VMEM((heads, tq, tq), F32), pltpu.VMEM((2, heads, tq, tq), F32),
               pltpu.VMEM((2, heads, 1, tq), F32), pltpu.VMEM((heads, 1, tq), F32),
               pltpu.VMEM((heads, HEAD_DIM + BF16_ROWS, tq), F32)]
    return pl.pallas_call(
        functools.partial(_attention_kernel, has_prefix=prefix is not None, multi=n_q > 1),
        grid=(b, N_HEADS // heads, n_q),
        in_specs=in_specs,
        out_specs=pl.BlockSpec((None, tq, width), lambda bi, h, i: (bi, i, h)),
        out_shape=jax.ShapeDtypeStruct((b, n, D_MODEL), BF16),
        scratch_shapes=scratch,
        compiler_params=pltpu.CompilerParams(dimension_semantics=("arbitrary",) * 3,
                                             vmem_limit_bytes=VMEM_LIMIT),
        name="attention" if prefix is not None else "attention_meta",
    )(*args)


def _mixer_out_kernel(x_ref, att_ref, yc_ref, gmix_ref, gffn_ref, gfin_ref, wg_ref, woc_ref, woa_ref,
                      wo_ref, wa_ref, wv_ref, wd_ref, fcw_ref, halo_ref, out_ref, a_scr, *, tail_at):
    t = x_ref.shape[0]

    @pl.when(pl.program_id(1) == 0)
    def _():
        a_scr[0:HALO, :] = halo_ref[...]

    x = x_ref[...]
    h = _rms_norm(x, gmix_ref[...]).astype(BF16)
    g_att = _sigmoid(_dot(h, wg_ref[:, 0:D_MODEL]))
    g_conv = _sigmoid(_dot(h, wg_ref[:, D_MODEL:2 * D_MODEL]))
    mix = g_att * _dot(att_ref[...], woa_ref[...]) + g_conv * _dot(yc_ref[...], woc_ref[...])
    z1 = x + _dot(mix.astype(BF16), wo_ref[...])
    h2 = _rms_norm(z1, gffn_ref[...]).astype(BF16)

    if tail_at is not None:
        a_scr[HALO:HALO + t, :] = _dot(h2, wa_ref[...])
        out_ref[...] = a_scr[tail_at:tail_at + HALO, :]
        return

    a_scr[HALO:HALO + t, :] = _dot(h2, wa_ref[...])
    conv = _causal_conv(a_scr, t, fcw_ref)
    act = conv * _sigmoid(conv) * _dot(h2, wv_ref[...])
    a_scr[0:HALO, :] = a_scr[t:t + HALO, :]
    out_ref[...] = _rms_norm(z1 + _dot(act.astype(BF16), wd_ref[...]), gfin_ref[...])


def _mixer_out(x, att, yc, g_mix, g_ffn, g_final, w_in, w_sq, w_ffn, w_d, ffn_conv_w, halo, *, tile, tail_at=None):
    b, n, _ = x.shape
    square = [_resident((None, D_MODEL, D_MODEL), (k, 0, 0)) for k in range(3)]
    halves = [_resident((D_MODEL, D_FF), (0, k)) for k in range(2)]
    rows = pl.BlockSpec((None, tile, D_MODEL), lambda bi, i: (bi, i, 0))
    if tail_at is not None:
        assert b == 1 and n == tile
        out_shape = jax.ShapeDtypeStruct((HALO, D_FF), F32)
        out_spec = pl.BlockSpec((HALO, D_FF), lambda bi, i: (0, 0))
    else:
        out_shape = jax.ShapeDtypeStruct(x.shape, F32)
        out_spec = rows
    consts = (g_mix, g_ffn, g_final, w_in, w_sq, w_sq, w_sq, w_ffn, w_ffn, w_d, ffn_conv_w, halo)
    const_specs = ([_resident(g_mix.shape)] * 3 + [_resident(*W_IN_BLOCKS["gates"])] + square + halves
                   + [_resident(w_d.shape), _resident(ffn_conv_w.shape), _resident(halo.shape)])
    return pl.pallas_call(
        functools.partial(_mixer_out_kernel, tail_at=tail_at),
        grid=(b, n // tile),
        in_specs=[rows, rows, rows] + const_specs,
        out_specs=out_spec,
        out_shape=out_shape,
        scratch_shapes=[pltpu.VMEM((tile + HALO, D_FF), F32)],
        compiler_params=pltpu.CompilerParams(dimension_semantics=("arbitrary", "arbitrary"),
                                             vmem_limit_bytes=VMEM_LIMIT),
        name="mixer_out" if tail_at is None else "mixer_out_meta",
    )(x, att, yc, *consts)


def kernel(x, meta_tokens, g_mix, w_in, b_f, conv_w, w_o_attn, w_o_conv, w_o, g_ffn, w_ffn_in, ffn_conv_w,
           w_ffn_out, g_final):
    assert w_in.shape[0] == 1, "one layer"
    d = D_MODEL
    w_in_packed, w_f = _repack_w_in(w_in[0].T)
    b_f_row = jnp.pad(b_f[0], (0, LANES - N_HEADS))[None, :]
    w_sq = jnp.stack([w_o_conv[0], w_o_attn[0], w_o[0]]).astype(BF16)
    w_ffn = w_ffn_in[0].astype(BF16)
    w_d = w_ffn_out[0].astype(BF16)
    g_mix_row, g_ffn_row, g_final_row = g_mix[0][None, :], g_ffn[0][None, :], g_final[None, :]
    in_consts = (g_mix_row, w_in_packed, w_f, b_f_row, conv_w[0])
    out_consts = (g_mix_row, g_ffn_row, g_final_row, w_in_packed, w_sq, w_ffn, w_d, ffn_conv_w[0])

    meta = jnp.pad(meta_tokens.astype(F32), ((0, META_TILE - N_META), (0, 0)))[None]
    qftm, kfm, vtm, ycm, cm, cu_tail = _mixer_in(
        meta, *in_consts, jnp.zeros((HALO, d), F32), jnp.zeros((1, LANES), F32), tile=META_TILE, tail_at=N_META)
    att_m = _attention(qftm, kfm, vtm, None, tq=META_TILE, heads=HEADS_PER_STEP)
    a_tail = _mixer_out(meta, att_m, ycm, *out_consts, jnp.zeros((HALO, D_FF), F32),
                        tile=META_TILE, tail_at=N_META)

    qft, kf, vt, yc = _mixer_in(x, *in_consts, cu_tail, cm[0, N_META - 1:N_META, :], tile=ROW_TILE_IN)
    att = _attention(qft, kf, vt, (kfm[0], vtm[0]), tq=Q_TILE, heads=HEADS_PER_STEP)
    return _mixer_out(x, att, yc, *out_consts, a_tail, tile=ROW_TILE_OUT)
```

```python
import functools

import jax
import jax.numpy as jnp
from jax import lax
from jax.experimental import pallas as pl
from jax.experimental.pallas import tpu as pltpu

D_MODEL = 1024
N_HEADS = 8
HEAD_DIM = 128
N_META = 16
D_FF = 2816
RMS_EPS = 1e-6

LANES = 128
SUBLANES = 8
BF16_ROWS = 2 * SUBLANES
HALO = SUBLANES
VMEM_LIMIT = 56 * 1024 * 1024

LOG2E = 1.4426950408889634
Q_SCALE = HEAD_DIM ** -0.5 * LOG2E
MASKED = -1e30

META_TILE = LANES
ROW_TILE_IN = 512
ROW_TILE_OUT = 512
Q_TILE = 512
HEADS_PER_STEP = 4
W_IN_BLOCKS = {
    "qkv": ((D_MODEL, 3 * D_MODEL), (0, 0)),
    "conv": ((D_MODEL, 3 * D_MODEL), (0, 1)),
    "gates": ((D_MODEL, 2 * D_MODEL), (0, 3)),
}

BF16 = jnp.bfloat16
F32 = jnp.float32


def _dot(a, b):
    return jnp.dot(a, b, preferred_element_type=F32)


def _rms_norm(x, g):
    return x * lax.rsqrt(jnp.mean(x * x, axis=-1, keepdims=True) + RMS_EPS) * g


def _sigmoid(x):
    return 1.0 / (1.0 + jnp.exp(-x))


def _split3(x):
    hi = x.astype(BF16).astype(F32)
    mid = (x - hi).astype(BF16).astype(F32)
    lo = (x - hi - mid).astype(BF16).astype(F32)
    return hi, mid, lo


def _cumsum_rows(x, start):
    t = x.shape[0]
    row = lax.broadcasted_iota(jnp.int32, (LANES, LANES), 0)
    col = lax.broadcasted_iota(jnp.int32, (LANES, LANES), 1)
    tri = jnp.where(col <= row, 1.0, 0.0).astype(BF16)
    blocks = []
    for r in range(0, t, LANES):
        hi, mid, lo = _split3(x[r:r + LANES])
        local = _dot(tri, hi.astype(BF16)) + _dot(tri, mid.astype(BF16)) + _dot(tri, lo.astype(BF16))
        blocks.append(local + start)
        start = blocks[-1][LANES - 1:LANES]
    return jnp.concatenate(blocks, axis=0), start


def _causal_conv(scr, t, w_ref):
    out = scr[HALO - 2:HALO - 2 + t, :] * w_ref[0:1, :]
    out = out + scr[HALO - 1:HALO - 1 + t, :] * w_ref[1:2, :]
    return out + scr[HALO:HALO + t, :] * w_ref[2:3, :]


def _resident(block_shape, block_index=None):
    index = tuple(block_index) if block_index is not None else (0,) * len(block_shape)
    return pl.BlockSpec(block_shape, lambda *_: index, pipeline_mode=pl.Buffered(1))


def _repack_w_in_kernel(w_ref, f_ref, o_ref, of_ref):
    o_ref[...] = w_ref[...].T.astype(BF16)
    f_rows = jnp.concatenate([f_ref[...], jnp.zeros((LANES - N_HEADS, D_MODEL), F32)], axis=0)
    of_ref[...] = f_rows.T.astype(BF16)


def _repack_w_in(w_t):
    d = D_MODEL
    assert N_HEADS % SUBLANES == 0
    return pl.pallas_call(
        _repack_w_in_kernel,
        grid=(8,),
        in_specs=[pl.BlockSpec((pl.Element(d), pl.Element(d)),
                               lambda j: ((j * (d // N_HEADS) + jnp.where(j >= 3, 1, 0)) * N_HEADS, 0)),
                  pl.BlockSpec((N_HEADS, d), lambda j: (3 * d // N_HEADS, 0))],
        out_specs=[pl.BlockSpec((d, d), lambda j: (0, j)), pl.BlockSpec((d, LANES), lambda j: (0, 0))],
        out_shape=[jax.ShapeDtypeStruct((d, 8 * d), BF16), jax.ShapeDtypeStruct((d, LANES), BF16)],
        compiler_params=pltpu.CompilerParams(dimension_semantics=("arbitrary",), vmem_limit_bytes=VMEM_LIMIT),
        name="repack_w_in",
    )(w_t, w_t)


def _bias_columns(c, head):
    lane = lax.broadcasted_iota(jnp.int32, c.shape, 1)
    pieces = _split3(c[:, head:head + 1])
    k_side = jnp.where((lane >= 3) & (lane < 6), 1.0, 0.0)
    q_side = jnp.where(lane < 3, 1.0, 0.0)
    for n, piece in enumerate(pieces):
        k_side = jnp.where(lane == n, -piece, k_side)
        q_side = jnp.where(lane == 3 + n, piece, q_side)
    return k_side, q_side


def _mixer_in_kernel(x_ref, g_ref, wqkv_ref, wf_ref, bf_ref, wcv_ref, cw_ref, halo_ref, c0_ref,
                     qft_ref, kf_ref, vt_ref, yc_ref, *rest, tail_at):
    if tail_at is not None:
        c_ref, tail_ref, cu_scr, carry = rest
    else:
        cu_scr, carry = rest
    t = x_ref.shape[0]

    @pl.when(pl.program_id(1) == 0)
    def _():
        cu_scr[0:HALO, :] = halo_ref[...]
        carry[...] = c0_ref[...]

    h = _rms_norm(x_ref[...], g_ref[...]).astype(BF16)

    gc = _dot(h, wcv_ref[:, D_MODEL:2 * D_MODEL])
    u = _dot(h, wcv_ref[:, 2 * D_MODEL:3 * D_MODEL])
    cu_scr[HALO:HALO + t, :] = gc * u
    conv = _causal_conv(cu_scr, t, cw_ref)
    if tail_at is not None:
        tail_ref[...] = cu_scr[tail_at:tail_at + HALO, :]
    cu_scr[0:HALO, :] = cu_scr[t:t + HALO, :]

    f = _dot(h, wf_ref[...]) + bf_ref[...]
    log_f = (jnp.minimum(f, 0.0) - jnp.log1p(jnp.exp(-jnp.abs(f)))) * LOG2E
    c, carry[...] = _cumsum_rows(log_f, carry[...])
    if tail_at is not None:
        c_ref[...] = c
    for head in range(N_HEADS):
        lo = head * 2 * HEAD_DIM + HEAD_DIM
        k_cols, q_cols = _bias_columns(c, head)
        kf_ref[:, lo:lo + HEAD_DIM] = k_cols.astype(BF16)
        qft_ref[lo:lo + HEAD_DIM, :] = q_cols.T.astype(BF16)

    vt_ref[...] = _dot(h, wqkv_ref[:, 2 * D_MODEL:3 * D_MODEL]).T.astype(BF16)
    q = _dot(h, wqkv_ref[:, 0:D_MODEL]) * Q_SCALE
    k = _dot(h, wqkv_ref[:, D_MODEL:2 * D_MODEL]).astype(BF16)
    for head in range(N_HEADS):
        src = slice(head * HEAD_DIM, (head + 1) * HEAD_DIM)
        dst = slice(head * 2 * HEAD_DIM, head * 2 * HEAD_DIM + HEAD_DIM)
        qft_ref[dst, :] = q[:, src].T.astype(BF16)
        kf_ref[:, dst] = k[:, src]
    yc_ref[...] = (_dot(h, wcv_ref[:, 0:D_MODEL]) * conv).astype(BF16)


def _mixer_in(x, g_mix, w_in, w_f, b_f, conv_w, halo, c0, *, tile, tail_at=None):
    b, n, _ = x.shape
    rows = pl.BlockSpec((None, tile, D_MODEL), lambda bi, i: (bi, i, 0))
    out_shape = [jax.ShapeDtypeStruct((b, 2 * D_MODEL, n), BF16), jax.ShapeDtypeStruct((b, n, 2 * D_MODEL), BF16),
                 jax.ShapeDtypeStruct((b, D_MODEL, n), BF16), jax.ShapeDtypeStruct((b, n, D_MODEL), BF16)]
    out_specs = [pl.BlockSpec((None, 2 * D_MODEL, tile), lambda bi, i: (bi, 0, i)),
                 pl.BlockSpec((None, tile, 2 * D_MODEL), lambda bi, i: (bi, i, 0)),
                 pl.BlockSpec((None, D_MODEL, tile), lambda bi, i: (bi, 0, i)), rows]
    if tail_at is not None:
        assert b == 1 and n == tile
        out_shape += [jax.ShapeDtypeStruct((b, n, LANES), F32), jax.ShapeDtypeStruct((HALO, D_MODEL), F32)]
        out_specs += [pl.BlockSpec((None, tile, LANES), lambda bi, i: (bi, i, 0)),
                      pl.BlockSpec((HALO, D_MODEL), lambda bi, i: (0, 0))]
    return pl.pallas_call(
        functools.partial(_mixer_in_kernel, tail_at=tail_at),
        grid=(b, n // tile),
        in_specs=[rows, _resident((1, D_MODEL)), _resident(*W_IN_BLOCKS["qkv"]), _resident(w_f.shape),
                  _resident(b_f.shape), _resident(*W_IN_BLOCKS["conv"]), _resident(conv_w.shape),
                  _resident(halo.shape), _resident(c0.shape)],
        out_specs=out_specs,
        out_shape=out_shape,
        scratch_shapes=[pltpu.VMEM((tile + HALO, D_MODEL), F32), pltpu.VMEM((1, LANES), F32)],
        compiler_params=pltpu.CompilerParams(dimension_semantics=("arbitrary", "arbitrary"),
                                             vmem_limit_bytes=VMEM_LIMIT),
        name="mixer_in" if tail_at is None else "mixer_in_meta",
    )(x, g_mix, w_in, w_f, b_f, w_in, conv_w, halo, c0)


def _attention_kernel(qft_ref, *rest, has_prefix, multi):
    rest = list(rest)
    qftn_ref = rest.pop(0) if multi else None
    kf_ref, vt_ref = rest.pop(0), rest.pop(0)
    kpf_ref, vtp_ref = (rest.pop(0), rest.pop(0)) if has_prefix else (None, None)
    o_ref, diag_scr, ab_scr, abmax_scr, m_scr, acc_scr = rest
    tq = qft_ref.shape[1]
    n_q = kf_ref.shape[0] // tq
    group = range(vt_ref.shape[0] // HEAD_DIM)
    i = pl.program_id(2)

    def cols(g):
        return slice(g * HEAD_DIM, (g + 1) * HEAD_DIM)

    def wide(g):
        return slice(g * 2 * HEAD_DIM, (g + 1) * 2 * HEAD_DIM)

    def rows_at(block):
        return pl.ds(pl.multiple_of(block * tq, tq), tq)

    def ones_rows(keys):
        first_row = lax.broadcasted_iota(jnp.int32, (BF16_ROWS, keys), 0) == 0
        return jnp.where(first_row, 1.0, 0.0).astype(BF16)

    key = lax.broadcasted_iota(jnp.int32, (tq, tq), 0)
    qry = lax.broadcasted_iota(jnp.int32, (tq, tq), 1)
    ones_tq = ones_rows(tq)

    def q_cur(g):
        return qft_ref[wide(g), :]

    def produce(dst, block, q, max_dst=None):
        for g in group:
            s = _dot(kf_ref[rows_at(block), wide(g)], q(g))
            dst[g] = s
            if max_dst is not None:
                max_dst[g] = jnp.max(s, axis=0, keepdims=True)

    def update(g, s, vt_ones, s_max):
        m_old = m_scr[g]
        m_new = jnp.maximum(m_old, s_max)
        alpha = jnp.exp2(m_old - m_new)
        p = jnp.exp2(s - m_new).astype(BF16)
        acc_scr[g] = alpha * acc_scr[g] + _dot(vt_ones, p)
        m_scr[g] = m_new

    def prefix_scores():
        return [_dot(kpf_ref[:, wide(g)], q_cur(g)) for g in group] if has_prefix else None

    def consume(src, block, max_src):
        for g in group:
            vt_ones = jnp.concatenate([vt_ref[cols(g), rows_at(block)], ones_tq], axis=0)
            update(g, src[g], vt_ones, max_src[g])

    def start(s_prefix):
        for g in group:
            s = jnp.where(key <= qry, diag_scr[g], MASKED)
            vt = vt_ref[cols(g), rows_at(i)]
            if has_prefix:
                s = jnp.concatenate([s, s_prefix[g][0:N_META]], axis=0)
                vt = jnp.concatenate([vt, vtp_ref[cols(g), :]], axis=1)
            s_max = jnp.max(s, axis=0, keepdims=True)
            p = jnp.exp2(s - s_max).astype(BF16)
            if has_prefix:
                p = jnp.concatenate([p, jnp.zeros((vt.shape[1] - p.shape[0], tq), BF16)], axis=0)
            acc_scr[g] = _dot(jnp.concatenate([vt, ones_rows(vt.shape[1])], axis=0), p)
            m_scr[g] = s_max

    @pl.when(i == 0)
    def _():
        produce(diag_scr, 0, q_cur)

    if not multi:
        start(prefix_scores())
    else:
        first = i == 0
        nxt = jnp.minimum(i + 1, n_q - 1)

        def q_nxt(g):
            return qftn_ref[wide(g), :]

        s_prefix = prefix_scores()
        slot = jnp.where(first, 0, (i - 1) & 1)
        produce(ab_scr.at[slot], jnp.where(first, nxt, 0),
                lambda g: jnp.where(first, q_nxt(g), q_cur(g)), abmax_scr.at[slot])
        start(s_prefix)

        @pl.when(first)
        def _():
            diag_scr[...] = ab_scr[0]

        def step(j, src, dst):
            produce(ab_scr.at[dst], j + 1, q_cur, abmax_scr.at[dst])
            consume(ab_scr.at[src], j, abmax_scr.at[src])

        @pl.when((i >= 2) & ((i & 1) == 0))
        def _():
            step(0, 1, 0)

        def pair(jj, carry):
            j = 1 - (i & 1) + 2 * jj
            step(j, 0, 1)
            step(j + 1, 1, 0)
            return carry

        lax.fori_loop(0, lax.shift_right_logical(jnp.maximum(i - 1, 0), 1), pair, 0)

        @pl.when(i >= 1)
        def _():
            produce(diag_scr, nxt, q_nxt)
            consume(ab_scr.at[0], i - 1, abmax_scr.at[0])

    for g in group:
        acc = acc_scr[g]
        inv_l = 1.0 / acc[HEAD_DIM:HEAD_DIM + 1]
        o_ref[:, cols(g)] = (acc[0:HEAD_DIM] * inv_l).T.astype(o_ref.dtype)


def _attention(qft, kf, vt, prefix, *, tq, heads):
    b, n, _ = kf.shape
    n_q = n // tq
    width = heads * HEAD_DIM
    in_specs, args = [pl.BlockSpec((None, 2 * width, tq), lambda bi, h, i: (bi, h, i))], [qft]
    if n_q > 1:
        in_specs.append(pl.BlockSpec((None, 2 * width, tq), lambda bi, h, i: (bi, h, jnp.minimum(i + 1, n_q - 1))))
        args.append(qft)
    in_specs += [pl.BlockSpec((None, n, 2 * width), lambda bi, h, i: (bi, 0, h)),
                 pl.BlockSpec((None, width, n), lambda bi, h, i: (bi, h, 0))]
    args += [kf, vt]
    if prefix is not None:
        kpf, vtp = prefix
        n_p = kpf.shape[0]
        in_specs += [pl.BlockSpec((n_p, 2 * width), lambda bi, h, i: (0, h)),
                     pl.BlockSpec((width, n_p), lambda bi, h, i: (h, 0))]
        args += [kpf, vtp]
    scratch = [pltpu.VMEM((heads, tq, tq), F32), pltpu.VMEM((2, heads, tq, tq), F32),
               pltpu.VMEM((2, heads, 1, tq), F32), pltpu.VMEM((heads, 1, tq), F32),
               pltpu.VMEM((heads, HEAD_DIM + BF16_ROWS, tq), F32)]
    return pl.pallas_call(
        functools.partial(_attention_kernel, has_prefix=prefix is not None, multi=n_q > 1),
        grid=(b, N_HEADS // heads, n_q),
        in_specs=in_specs,
        out_specs=pl.BlockSpec((None, tq, width), lambda bi, h, i: (bi, i, h)),
        out_shape=jax.ShapeDtypeStruct((b, n, D_MODEL), BF16),
        scratch_shapes=scratch,
        compiler_params=pltpu.CompilerParams(dimension_semantics=("arbitrary",) * 3,
                                             vmem_limit_bytes=VMEM_LIMIT),
        name="attention" if prefix is not None else "attention_meta",
    )(*args)


def _mixer_out_kernel(x_ref, att_ref, yc_ref, gmix_ref, gffn_ref, gfin_ref, wg_ref, woc_ref, woa_ref,
                      wo_ref, wa_ref, wv_ref, wd_ref, fcw_ref, halo_ref, out_ref, a_scr, *, tail_at):
    t = x_ref.shape[0]

    @pl.when(pl.program_id(1) == 0)
    def _():
        a_scr[0:HALO, :] = halo_ref[...]

    x = x_ref[...]
    h = _rms_norm(x, gmix_ref[...]).astype(BF16)
    g_att = _sigmoid(_dot(h, wg_ref[:, 0:D_MODEL]))
    g_conv = _sigmoid(_dot(h, wg_ref[:, D_MODEL:2 * D_MODEL]))
    mix = g_att * _dot(att_ref[...], woa_ref[...]) + g_conv * _dot(yc_ref[...], woc_ref[...])
    z1 = x + _dot(mix.astype(BF16), wo_ref[...])
    h2 = _rms_norm(z1, gffn_ref[...]).astype(BF16)

    if tail_at is not None:
        a_scr[HALO:HALO + t, :] = _dot(h2, wa_ref[...])
        out_ref[...] = a_scr[tail_at:tail_at + HALO, :]
        return

    a_scr[HALO:HALO + t, :] = _dot(h2, wa_ref[...])
    conv = _causal_conv(a_scr, t, fcw_ref)
    act = conv * _sigmoid(conv) * _dot(h2, wv_ref[...])
    a_scr[0:HALO, :] = a_scr[t:t + HALO, :]
    out_ref[...] = _rms_norm(z1 + _dot(act.astype(BF16), wd_ref[...]), gfin_ref[...])


def _mixer_out(x, att, yc, g_mix, g_ffn, g_final, w_in, w_sq, w_ffn, w_d, ffn_conv_w, halo, *, tile, tail_at=None):
    b, n, _ = x.shape
    square = [_resident((None, D_MODEL, D_MODEL), (k, 0, 0)) for k in range(3)]
    halves = [_resident((D_MODEL, D_FF), (0, k)) for k in range(2)]
    rows = pl.BlockSpec((None, tile, D_MODEL), lambda bi, i: (bi, i, 0))
    if tail_at is not None:
        assert b == 1 and n == tile
        out_shape = jax.ShapeDtypeStruct((HALO, D_FF), F32)
        out_spec = pl.BlockSpec((HALO, D_FF), lambda bi, i: (0, 0))
    else:
        out_shape = jax.ShapeDtypeStruct(x.shape, F32)
        out_spec = rows
    consts = (g_mix, g_ffn, g_final, w_in, w_sq, w_sq, w_sq, w_ffn, w_ffn, w_d, ffn_conv_w, halo)
    const_specs = ([_resident(g_mix.shape)] * 3 + [_resident(*W_IN_BLOCKS["gates"])] + square + halves
                   + [_resident(w_d.shape), _resident(ffn_conv_w.shape), _resident(halo.shape)])
    return pl.pallas_call(
        functools.partial(_mixer_out_kernel, tail_at=tail_at),
        grid=(b, n // tile),
        in_specs=[rows, rows, rows] + const_specs,
        out_specs=out_spec,
        out_shape=out_shape,
        scratch_shapes=[pltpu.VMEM((tile + HALO, D_FF), F32)],
        compiler_params=pltpu.CompilerParams(dimension_semantics=("arbitrary", "arbitrary"),
                                             vmem_limit_bytes=VMEM_LIMIT),
        name="mixer_out" if tail_at is None else "mixer_out_meta",
    )(x, att, yc, *consts)


def kernel(x, meta_tokens, g_mix, w_in, b_f, conv_w, w_o_attn, w_o_conv, w_o, g_ffn, w_ffn_in, ffn_conv_w,
           w_ffn_out, g_final):
    assert w_in.shape[0] == 1, "one layer"
    d = D_MODEL
    w_in_packed, w_f = _repack_w_in(w_in[0].T)
    b_f_row = jnp.pad(b_f[0], (0, LANES - N_HEADS))[None, :]
    w_sq = jnp.stack([w_o_conv[0], w_o_attn[0], w_o[0]]).astype(BF16)
    w_ffn = w_ffn_in[0].astype(BF16)
    w_d = w_ffn_out[0].astype(BF16)
    g_mix_row, g_ffn_row, g_final_row = g_mix[0][None, :], g_ffn[0][None, :], g_final[None, :]
    in_consts = (g_mix_row, w_in_packed, w_f, b_f_row, conv_w[0])
    out_consts = (g_mix_row, g_ffn_row, g_final_row, w_in_packed, w_sq, w_ffn, w_d, ffn_conv_w[0])

    meta = jnp.pad(meta_tokens.astype(F32), ((0, META_TILE - N_META), (0, 0)))[None]
    qftm, kfm, vtm, ycm, cm, cu_tail = _mixer_in(
        meta, *in_consts, jnp.zeros((HALO, d), F32), jnp.zeros((1, LANES), F32), tile=META_TILE, tail_at=N_META)
    att_m = _attention(qftm, kfm, vtm, None, tq=META_TILE, heads=HEADS_PER_STEP)
    a_tail = _mixer_out(meta, att_m, ycm, *out_consts, jnp.zeros((HALO, D_FF), F32),
                        tile=META_TILE, tail_at=N_META)

    qft, kf, vt, yc = _mixer_in(x, *in_consts, cu_tail, cm[0, N_META - 1:N_META, :], tile=ROW_TILE_IN)
    att = _attention(qft, kf, vt, (kfm[0], vtm[0]), tq=Q_TILE, heads=HEADS_PER_STEP)
    return _mixer_out(x, att, yc, *out_consts, a_tail, tile=ROW_TILE_OUT)
```

```python
import functools

import jax
import jax.numpy as jnp
from jax import lax
from jax.experimental import pallas as pl
from jax.experimental.pallas import tpu as pltpu

D_MODEL = 1024
N_HEADS = 8
HEAD_DIM = 128
N_META = 16
D_FF = 2816
RMS_EPS = 1e-6

LANES = 128
SUBLANES = 8
BF16_ROWS = 2 * SUBLANES
HALO = SUBLANES
VMEM_LIMIT = 56 * 1024 * 1024

LOG2E = 1.4426950408889634
Q_SCALE = HEAD_DIM ** -0.5 * LOG2E
MASKED = -1e30

META_TILE = LANES
ROW_TILE_IN = 512
ROW_TILE_OUT = 512
Q_TILE = 512
HEADS_PER_STEP = 4
W_IN_BLOCKS = {
    "qkv": ((D_MODEL, 3 * D_MODEL), (0, 0)),
    "conv": ((D_MODEL, 3 * D_MODEL), (0, 1)),
    "gates": ((D_MODEL, 2 * D_MODEL), (0, 3)),
}

BF16 = jnp.bfloat16
F32 = jnp.float32


def _dot(a, b):
    return jnp.dot(a, b, preferred_element_type=F32)


def _rms_norm(x, g):
    return x * lax.rsqrt(jnp.mean(x * x, axis=-1, keepdims=True) + RMS_EPS) * g


def _sigmoid(x):
    return 1.0 / (1.0 + jnp.exp(-x))


def _split3(x):
    hi = x.astype(BF16).astype(F32)
    mid = (x - hi).astype(BF16).astype(F32)
    lo = (x - hi - mid).astype(BF16).astype(F32)
    return hi, mid, lo


def _cumsum_rows(x, start):
    t = x.shape[0]
    row = lax.broadcasted_iota(jnp.int32, (LANES, LANES), 0)
    col = lax.broadcasted_iota(jnp.int32, (LANES, LANES), 1)
    tri = jnp.where(col <= row, 1.0, 0.0).astype(BF16)
    blocks = []
    for r in range(0, t, LANES):
        hi, mid, lo = _split3(x[r:r + LANES])
        local = _dot(tri, hi.astype(BF16)) + _dot(tri, mid.astype(BF16)) + _dot(tri, lo.astype(BF16))
        blocks.append(local + start)
        start = blocks[-1][LANES - 1:LANES]
    return jnp.concatenate(blocks, axis=0), start


def _causal_conv(scr, t, w_ref):
    out = scr[HALO - 2:HALO - 2 + t, :] * w_ref[0:1, :]
    out = out + scr[HALO - 1:HALO - 1 + t, :] * w_ref[1:2, :]
    return out + scr[HALO:HALO + t, :] * w_ref[2:3, :]


def _resident(block_shape, block_index=None):
    index = tuple(block_index) if block_index is not None else (0,) * len(block_shape)
    return pl.BlockSpec(block_shape, lambda *_: index, pipeline_mode=pl.Buffered(1))


def _repack_w_in_kernel(w_ref, f_ref, o_ref, of_ref):
    o_ref[...] = w_ref[...].T.astype(BF16)
    f_rows = jnp.concatenate([f_ref[...], jnp.zeros((LANES - N_HEADS, D_MODEL), F32)], axis=0)
    of_ref[...] = f_rows.T.astype(BF16)


def _repack_w_in(w_t):
    d = D_MODEL
    assert N_HEADS % SUBLANES == 0
    return pl.pallas_call(
        _repack_w_in_kernel,
        grid=(8,),
        in_specs=[pl.BlockSpec((pl.Element(d), pl.Element(d)),
                               lambda j: ((j * (d // N_HEADS) + jnp.where(j >= 3, 1, 0)) * N_HEADS, 0)),
                  pl.BlockSpec((N_HEADS, d), lambda j: (3 * d // N_HEADS, 0))],
        out_specs=[pl.BlockSpec((d, d), lambda j: (0, j)), pl.BlockSpec((d, LANES), lambda j: (0, 0))],
        out_shape=[jax.ShapeDtypeStruct((d, 8 * d), BF16), jax.ShapeDtypeStruct((d, LANES), BF16)],
        compiler_params=pltpu.CompilerParams(dimension_semantics=("arbitrary",), vmem_limit_bytes=VMEM_LIMIT),
        name="repack_w_in",
    )(w_t, w_t)


def _bias_columns(c, head):
    lane = lax.broadcasted_iota(jnp.int32, c.shape, 1)
    pieces = _split3(c[:, head:head + 1])
    k_side = jnp.where((lane >= 3) & (lane < 6), 1.0, 0.0)
    q_side = jnp.where(lane < 3, 1.0, 0.0)
    for n, piece in enumerate(pieces):
        k_side = jnp.where(lane == n, -piece, k_side)
        q_side = jnp.where(lane == 3 + n, piece, q_side)
    return k_side, q_side


def _mixer_in_kernel(x_ref, g_ref, wqkv_ref, wf_ref, bf_ref, wcv_ref, cw_ref, halo_ref, c0_ref,
                     qft_ref, kf_ref, vt_ref, yc_ref, *rest, tail_at):
    if tail_at is not None:
        c_ref, tail_ref, cu_scr, carry = rest
    else:
        cu_scr, carry = rest
    t = x_ref.shape[0]

    @pl.when(pl.program_id(1) == 0)
    def _():
        cu_scr[0:HALO, :] = halo_ref[...]
        carry[...] = c0_ref[...]

    h = _rms_norm(x_ref[...], g_ref[...]).astype(BF16)

    gc = _dot(h, wcv_ref[:, D_MODEL:2 * D_MODEL])
    u = _dot(h, wcv_ref[:, 2 * D_MODEL:3 * D_MODEL])
    cu_scr[HALO:HALO + t, :] = gc * u
    conv = _causal_conv(cu_scr, t, cw_ref)
    if tail_at is not None:
        tail_ref[...] = cu_scr[tail_at:tail_at + HALO, :]
    cu_scr[0:HALO, :] = cu_scr[t:t + HALO, :]

    f = _dot(h, wf_ref[...]) + bf_ref[...]
    log_f = (jnp.minimum(f, 0.0) - jnp.log1p(jnp.exp(-jnp.abs(f)))) * LOG2E
    c, carry[...] = _cumsum_rows(log_f, carry[...])
    if tail_at is not None:
        c_ref[...] = c
    for head in range(N_HEADS):
        lo = head * 2 * HEAD_DIM + HEAD_DIM
        k_cols, q_cols = _bias_columns(c, head)
        kf_ref[:, lo:lo + HEAD_DIM] = k_cols.astype(BF16)
        qft_ref[lo:lo + HEAD_DIM, :] = q_cols.T.astype(BF16)

    vt_ref[...] = _dot(h, wqkv_ref[:, 2 * D_MODEL:3 * D_MODEL]).T.astype(BF16)
    q = _dot(h, wqkv_ref[:, 0:D_MODEL]) * Q_SCALE
    k = _dot(h, wqkv_ref[:, D_MODEL:2 * D_MODEL]).astype(BF16)
    for head in range(N_HEADS):
        src = slice(head * HEAD_DIM, (head + 1) * HEAD_DIM)
        dst = slice(head * 2 * HEAD_DIM, head * 2 * HEAD_DIM + HEAD_DIM)
        qft_ref[dst, :] = q[:, src].T.astype(BF16)
        kf_ref[:, dst] = k[:, src]
    yc_ref[...] = (_dot(h, wcv_ref[:, 0:D_MODEL]) * conv).astype(BF16)


def _mixer_in(x, g_mix, w_in, w_f, b_f, conv_w, halo, c0, *, tile, tail_at=None):
    b, n, _ = x.shape
    rows = pl.BlockSpec((None, tile, D_MODEL), lambda bi, i: (bi, i, 0))
    out_shape = [jax.ShapeDtypeStruct((b, 2 * D_MODEL, n), BF16), jax.ShapeDtypeStruct((b, n, 2 * D_MODEL), BF16),
                 jax.ShapeDtypeStruct((b, D_MODEL, n), BF16), jax.ShapeDtypeStruct((b, n, D_MODEL), BF16)]
    out_specs = [pl.BlockSpec((None, 2 * D_MODEL, tile), lambda bi, i: (bi, 0, i)),
                 pl.BlockSpec((None, tile, 2 * D_MODEL), lambda bi, i: (bi, i, 0)),
                 pl.BlockSpec((None, D_MODEL, tile), lambda bi, i: (bi, 0, i)), rows]
    if tail_at is not None:
        assert b == 1 and n == tile
        out_shape += [jax.ShapeDtypeStruct((b, n, LANES), F32), jax.ShapeDtypeStruct((HALO, D_MODEL), F32)]
        out_specs += [pl.BlockSpec((None, tile, LANES), lambda bi, i: (bi, i, 0)),
                      pl.BlockSpec((HALO, D_MODEL), lambda bi, i: (0, 0))]
    return pl.pallas_call(
        functools.partial(_mixer_in_kernel, tail_at=tail_at),
        grid=(b, n // tile),
        in_specs=[rows, _resident((1, D_MODEL)), _resident(*W_IN_BLOCKS["qkv"]), _resident(w_f.shape),
                  _resident(b_f.shape), _resident(*W_IN_BLOCKS["conv"]), _resident(conv_w.shape),
                  _resident(halo.shape), _resident(c0.shape)],
        out_specs=out_specs,
        out_shape=out_shape,
        scratch_shapes=[pltpu.VMEM((tile + HALO, D_MODEL), F32), pltpu.VMEM((1, LANES), F32)],
        compiler_params=pltpu.CompilerParams(dimension_semantics=("arbitrary", "arbitrary"),
                                             vmem_limit_bytes=VMEM_LIMIT),
        name="mixer_in" if tail_at is None else "mixer_in_meta",
    )(x, g_mix, w_in, w_f, b_f, w_in, conv_w, halo, c0)


def _attention_kernel(qft_ref, *rest, has_prefix, multi):
    rest = list(rest)
    qftn_ref = rest.pop(0) if multi else None
    kf_ref, vt_ref = rest.pop(0), rest.pop(0)
    kpf_ref, vtp_ref = (rest.pop(0), rest.pop(0)) if has_prefix else (None, None)
    o_ref, diag_scr, ab_scr, abmax_scr, m_scr, acc_scr = rest
    tq = qft_ref.shape[1]
    n_q = kf_ref.shape[0] // tq
    group = range(vt_ref.shape[0] // HEAD_DIM)
    i = pl.program_id(2)

    def cols(g):
        return slice(g * HEAD_DIM, (g + 1) * HEAD_DIM)

    def wide(g):
        return slice(g * 2 * HEAD_DIM, (g + 1) * 2 * HEAD_DIM)

    def rows_at(block):
        if isinstance(block, int):
            return slice(block * tq, (block + 1) * tq)
        return pl.ds(pl.multiple_of(block * tq, tq), tq)

    def ones_rows(keys):
        first_row = lax.broadcasted_iota(jnp.int32, (BF16_ROWS, keys), 0) == 0
        return jnp.where(first_row, 1.0, 0.0).astype(BF16)

    key = lax.broadcasted_iota(jnp.int32, (tq, tq), 0)
    qry = lax.broadcasted_iota(jnp.int32, (tq, tq), 1)
    ones_tq = ones_rows(tq)

    def q_cur(g):
        return qft_ref[wide(g), :]

    def produce(dst, block, q, max_dst=None):
        for g in group:
            s = _dot(kf_ref[rows_at(block), wide(g)], q(g))
            dst[g] = s
            if max_dst is not None:
                max_dst[g] = jnp.max(s, axis=0, keepdims=True)

    def update(g, s, vt_ones, s_max):
        m_old = m_scr[g]
        m_new = jnp.maximum(m_old, s_max)
        alpha = jnp.exp2(m_old - m_new)
        p = jnp.exp2(s - m_new).astype(BF16)
        acc_scr[g] = alpha * acc_scr[g] + _dot(vt_ones, p)
        m_scr[g] = m_new

    def prefix_scores():
        return [_dot(kpf_ref[:, wide(g)], q_cur(g)) for g in group] if has_prefix else None

    def consume(src, block, max_src):
        for g in group:
            vt_ones = jnp.concatenate([vt_ref[cols(g), rows_at(block)], ones_tq], axis=0)
            update(g, src[g], vt_ones, max_src[g])

    def start(s_prefix, block, diag):
        for g in group:
            s = jnp.where(key <= qry, diag[g], MASKED)
            vt = vt_ref[cols(g), rows_at(block)]
            if has_prefix:
                s = jnp.concatenate([s, s_prefix[g][0:N_META]], axis=0)
                vt = jnp.concatenate([vt, vtp_ref[cols(g), :]], axis=1)
            s_max = jnp.max(s, axis=0, keepdims=True)
            p = jnp.exp2(s - s_max).astype(BF16)
            if has_prefix:
                p = jnp.concatenate([p, jnp.zeros((vt.shape[1] - p.shape[0], tq), BF16)], axis=0)
            acc_scr[g] = _dot(jnp.concatenate([vt, ones_rows(vt.shape[1])], axis=0), p)
            m_scr[g] = s_max

    @pl.when(i == 0)
    def _():
        produce(diag_scr.at[0], 0, q_cur)

    if not multi:
        start(prefix_scores(), 0, diag_scr.at[0])
    else:
        def q_nxt(g):
            return qftn_ref[wide(g), :]

        def whole_step(blk):
            nxt = min(blk + 1, n_q - 1)
            s_prefix = prefix_scores()
            if blk == 0:
                produce(diag_scr.at[1], nxt, q_nxt)
                start(s_prefix, blk, diag_scr.at[0])
                return
            produce(ab_scr.at[0], 0, q_cur, abmax_scr.at[0])
            start(s_prefix, blk, diag_scr.at[blk & 1])
            for j in range(blk - 1):
                produce(ab_scr.at[(j + 1) & 1], j + 1, q_cur, abmax_scr.at[(j + 1) & 1])
                consume(ab_scr.at[j & 1], j, abmax_scr.at[j & 1])
            produce(diag_scr.at[(blk + 1) & 1], nxt, q_nxt)
            consume(ab_scr.at[(blk - 1) & 1], blk - 1, abmax_scr.at[(blk - 1) & 1])

        lax.switch(i, [functools.partial(whole_step, blk) for blk in range(n_q)])

    for g in group:
        acc = acc_scr[g]
        inv_l = 1.0 / acc[HEAD_DIM:HEAD_DIM + 1]
        o_ref[:, cols(g)] = (acc[0:HEAD_DIM] * inv_l).T.astype(o_ref.dtype)


def _attention(qft, kf, vt, prefix, *, tq, heads):
    b, n, _ = kf.shape
    n_q = n // tq
    width = heads * HEAD_DIM
    in_specs, args = [pl.BlockSpec((None, 2 * width, tq), lambda bi, h, i: (bi, h, i))], [qft]
    if n_q > 1:
        in_specs.append(pl.BlockSpec((None, 2 * width, tq), lambda bi, h, i: (bi, h, jnp.minimum(i + 1, n_q - 1))))
        args.append(qft)
    in_specs += [pl.BlockSpec((None, n, 2 * width), lambda bi, h, i: (bi, 0, h)),
                 pl.BlockSpec((None, width, n), lambda bi, h, i: (bi, h, 0))]
    args += [kf, vt]
    if prefix is not None:
        kpf, vtp = prefix
        n_p = kpf.shape[0]
        in_specs += [pl.BlockSpec((n_p, 2 * width), lambda bi, h, i: (0, h)),
                     pl.BlockSpec((width, n_p), lambda bi, h, i: (h, 0))]
        args += [kpf, vtp]
    scratch = [pltpu.VMEM((2, heads, tq, tq), F32), pltpu.VMEM((2, heads, tq, tq), F32),
               pltpu.VMEM((2, heads, 1, tq), F32), pltpu.VMEM((heads, 1, tq), F32),
               pltpu.VMEM((heads, HEAD_DIM + BF16_ROWS, tq), F32)]
    return pl.pallas_call(
        functools.partial(_attention_kernel, has_prefix=prefix is not None, multi=n_q > 1),
        grid=(b, N_HEADS // heads, n_q),
        in_specs=in_specs,
        out_specs=pl.BlockSpec((None, tq, width), lambda bi, h, i: (bi, i, h)),
        out_shape=jax.ShapeDtypeStruct((b, n, D_MODEL), BF16),
        scratch_shapes=scratch,
        compiler_params=pltpu.CompilerParams(dimension_semantics=("arbitrary",) * 3,
                                             vmem_limit_bytes=VMEM_LIMIT),
        name="attention" if prefix is not None else "attention_meta",
    )(*args)


def _mixer_out_kernel(x_ref, att_ref, yc_ref, gmix_ref, gffn_ref, gfin_ref, wg_ref, woc_ref, woa_ref,
                      wo_ref, wa_ref, wv_ref, wd_ref, fcw_ref, halo_ref, out_ref, a_scr, *, tail_at):
    t = x_ref.shape[0]

    @pl.when(pl.program_id(1) == 0)
    def _():
        a_scr[0:HALO, :] = halo_ref[...]

    x = x_ref[...]
    h = _rms_norm(x, gmix_ref[...]).astype(BF16)
    g_att = _sigmoid(_dot(h, wg_ref[:, 0:D_MODEL]))
    g_conv = _sigmoid(_dot(h, wg_ref[:, D_MODEL:2 * D_MODEL]))
    mix = g_att * _dot(att_ref[...], woa_ref[...]) + g_conv * _dot(yc_ref[...], woc_ref[...])
    z1 = x + _dot(mix.astype(BF16), wo_ref[...])
    h2 = _rms_norm(z1, gffn_ref[...]).astype(BF16)

    if tail_at is not None:
        a_scr[HALO:HALO + t, :] = _dot(h2, wa_ref[...])
        out_ref[...] = a_scr[tail_at:tail_at + HALO, :]
        return

    a_scr[HALO:HALO + t, :] = _dot(h2, wa_ref[...])
    conv = _causal_conv(a_scr, t, fcw_ref)
    act = conv * _sigmoid(conv) * _dot(h2, wv_ref[...])
    a_scr[0:HALO, :] = a_scr[t:t + HALO, :]
    out_ref[...] = _rms_norm(z1 + _dot(act.astype(BF16), wd_ref[...]), gfin_ref[...])


def _mixer_out(x, att, yc, g_mix, g_ffn, g_final, w_in, w_sq, w_ffn, w_d, ffn_conv_w, halo, *, tile, tail_at=None):
    b, n, _ = x.shape
    square = [_resident((None, D_MODEL, D_MODEL), (k, 0, 0)) for k in range(3)]
    halves = [_resident((D_MODEL, D_FF), (0, k)) for k in range(2)]
    rows = pl.BlockSpec((None, tile, D_MODEL), lambda bi, i: (bi, i, 0))
    if tail_at is not None:
        assert b == 1 and n == tile
        out_shape = jax.ShapeDtypeStruct((HALO, D_FF), F32)
        out_spec = pl.BlockSpec((HALO, D_FF), lambda bi, i: (0, 0))
    else:
        out_shape = jax.ShapeDtypeStruct(x.shape, F32)
        out_spec = rows
    consts = (g_mix, g_ffn, g_final, w_in, w_sq, w_sq, w_sq, w_ffn, w_ffn, w_d, ffn_conv_w, halo)
    const_specs = ([_resident(g_mix.shape)] * 3 + [_resident(*W_IN_BLOCKS["gates"])] + square + halves
                   + [_resident(w_d.shape), _resident(ffn_conv_w.shape), _resident(halo.shape)])
    return pl.pallas_call(
        functools.partial(_mixer_out_kernel, tail_at=tail_at),
        grid=(b, n // tile),
        in_specs=[rows, rows, rows] + const_specs,
        out_specs=out_spec,
        out_shape=out_shape,
        scratch_shapes=[pltpu.VMEM((tile + HALO, D_FF), F32)],
        compiler_params=pltpu.CompilerParams(dimension_semantics=("arbitrary", "arbitrary"),
                                             vmem_limit_bytes=VMEM_LIMIT),
        name="mixer_out" if tail_at is None else "mixer_out_meta",
    )(x, att, yc, *consts)


def kernel(x, meta_tokens, g_mix, w_in, b_f, conv_w, w_o_attn, w_o_conv, w_o, g_ffn, w_ffn_in, ffn_conv_w,
           w_ffn_out, g_final):
    assert w_in.shape[0] == 1, "one layer"
    d = D_MODEL
    w_in_packed, w_f = _repack_w_in(w_in[0].T)
    b_f_row = jnp.pad(b_f[0], (0, LANES - N_HEADS))[None, :]
    w_sq = jnp.stack([w_o_conv[0], w_o_attn[0], w_o[0]]).astype(BF16)
    w_ffn = w_ffn_in[0].astype(BF16)
    w_d = w_ffn_out[0].astype(BF16)
    g_mix_row, g_ffn_row, g_final_row = g_mix[0][None, :], g_ffn[0][None, :], g_final[None, :]
    in_consts = (g_mix_row, w_in_packed, w_f, b_f_row, conv_w[0])
    out_consts = (g_mix_row, g_ffn_row, g_final_row, w_in_packed, w_sq, w_ffn, w_d, ffn_conv_w[0])

    meta = jnp.pad(meta_tokens.astype(F32), ((0, META_TILE - N_META), (0, 0)))[None]
    qftm, kfm, vtm, ycm, cm, cu_tail = _mixer_in(
        meta, *in_consts, jnp.zeros((HALO, d), F32), jnp.zeros((1, LANES), F32), tile=META_TILE, tail_at=N_META)
    att_m = _attention(qftm, kfm, vtm, None, tq=META_TILE, heads=HEADS_PER_STEP)
    a_tail = _mixer_out(meta, att_m, ycm, *out_consts, jnp.zeros((HALO, D_FF), F32),
                        tile=META_TILE, tail_at=N_META)

    qft, kf, vt, yc = _mixer_in(x, *in_consts, cu_tail, cm[0, N_META - 1:N_META, :], tile=ROW_TILE_IN)
    att = _attention(qft, kf, vt, (kfm[0], vtm[0]), tq=Q_TILE, heads=HEADS_PER_STEP)
    return _mixer_out(x, att, yc, *out_consts, a_tail, tile=ROW_TILE_OUT)
```

```python
import functools

import jax
import jax.numpy as jnp
from jax import lax
from jax.experimental import pallas as pl
from jax.experimental.pallas import tpu as pltpu

D_MODEL = 1024
N_HEADS = 8
HEAD_DIM = 128
N_META = 16
D_FF = 2816
RMS_EPS = 1e-6

LANES = 128
SUBLANES = 8
BF16_ROWS = 2 * SUBLANES
HALO = SUBLANES
VMEM_LIMIT = 56 * 1024 * 1024

LOG2E = 1.4426950408889634
Q_SCALE = HEAD_DIM ** -0.5 * LOG2E
MASKED = -1e30

META_TILE = LANES
ROW_TILE_IN = 512
ROW_TILE_OUT = 512
Q_TILE = 512
HEADS_PER_STEP = 4
UNROLLED_BLOCKS = 3
W_IN_BLOCKS = {
    "qkv": ((D_MODEL, 3 * D_MODEL), (0, 0)),
    "conv": ((D_MODEL, 3 * D_MODEL), (0, 1)),
    "gates": ((D_MODEL, 2 * D_MODEL), (0, 3)),
}

BF16 = jnp.bfloat16
F32 = jnp.float32


def _dot(a, b):
    return jnp.dot(a, b, preferred_element_type=F32)


def _rms_norm(x, g):
    return x * lax.rsqrt(jnp.mean(x * x, axis=-1, keepdims=True) + RMS_EPS) * g


def _sigmoid(x):
    return 1.0 / (1.0 + jnp.exp(-x))


def _split3(x):
    hi = x.astype(BF16).astype(F32)
    mid = (x - hi).astype(BF16).astype(F32)
    lo = (x - hi - mid).astype(BF16).astype(F32)
    return hi, mid, lo


def _cumsum_rows(x, start):
    t = x.shape[0]
    row = lax.broadcasted_iota(jnp.int32, (LANES, LANES), 0)
    col = lax.broadcasted_iota(jnp.int32, (LANES, LANES), 1)
    tri = jnp.where(col <= row, 1.0, 0.0).astype(BF16)
    blocks = []
    for r in range(0, t, LANES):
        hi, mid, lo = _split3(x[r:r + LANES])
        local = _dot(tri, hi.astype(BF16)) + _dot(tri, mid.astype(BF16)) + _dot(tri, lo.astype(BF16))
        blocks.append(local + start)
        start = blocks[-1][LANES - 1:LANES]
    return jnp.concatenate(blocks, axis=0), start


def _causal_conv(scr, t, w_ref):
    out = scr[HALO - 2:HALO - 2 + t, :] * w_ref[0:1, :]
    out = out + scr[HALO - 1:HALO - 1 + t, :] * w_ref[1:2, :]
    return out + scr[HALO:HALO + t, :] * w_ref[2:3, :]


def _resident(block_shape, block_index=None):
    index = tuple(block_index) if block_index is not None else (0,) * len(block_shape)
    return pl.BlockSpec(block_shape, lambda *_: index, pipeline_mode=pl.Buffered(1))


def _repack_w_in_kernel(w_ref, f_ref, o_ref, of_ref):
    o_ref[...] = w_ref[...].T.astype(BF16)
    f_rows = jnp.concatenate([f_ref[...], jnp.zeros((LANES - N_HEADS, D_MODEL), F32)], axis=0)
    of_ref[...] = f_rows.T.astype(BF16)


def _repack_w_in(w_t):
    d = D_MODEL
    assert N_HEADS % SUBLANES == 0
    return pl.pallas_call(
        _repack_w_in_kernel,
        grid=(8,),
        in_specs=[pl.BlockSpec((pl.Element(d), pl.Element(d)),
                               lambda j: ((j * (d // N_HEADS) + jnp.where(j >= 3, 1, 0)) * N_HEADS, 0)),
                  pl.BlockSpec((N_HEADS, d), lambda j: (3 * d // N_HEADS, 0))],
        out_specs=[pl.BlockSpec((d, d), lambda j: (0, j)), pl.BlockSpec((d, LANES), lambda j: (0, 0))],
        out_shape=[jax.ShapeDtypeStruct((d, 8 * d), BF16), jax.ShapeDtypeStruct((d, LANES), BF16)],
        compiler_params=pltpu.CompilerParams(dimension_semantics=("arbitrary",), vmem_limit_bytes=VMEM_LIMIT),
        name="repack_w_in",
    )(w_t, w_t)


def _bias_columns(c, head):
    lane = lax.broadcasted_iota(jnp.int32, c.shape, 1)
    pieces = _split3(c[:, head:head + 1])
    k_side = jnp.where((lane >= 3) & (lane < 6), 1.0, 0.0)
    q_side = jnp.where(lane < 3, 1.0, 0.0)
    for n, piece in enumerate(pieces):
        k_side = jnp.where(lane == n, -piece, k_side)
        q_side = jnp.where(lane == 3 + n, piece, q_side)
    return k_side, q_side


def _mixer_in_kernel(x_ref, g_ref, wqkv_ref, wf_ref, bf_ref, wcv_ref, cw_ref, halo_ref, c0_ref,
                     qft_ref, kf_ref, vt_ref, yc_ref, *rest, tail_at):
    if tail_at is not None:
        c_ref, tail_ref, cu_scr, carry = rest
    else:
        cu_scr, carry = rest
    t = x_ref.shape[0]

    @pl.when(pl.program_id(1) == 0)
    def _():
        cu_scr[0:HALO, :] = halo_ref[...]
        carry[...] = c0_ref[...]

    h = _rms_norm(x_ref[...], g_ref[...]).astype(BF16)

    gc = _dot(h, wcv_ref[:, D_MODEL:2 * D_MODEL])
    u = _dot(h, wcv_ref[:, 2 * D_MODEL:3 * D_MODEL])
    cu_scr[HALO:HALO + t, :] = gc * u
    conv = _causal_conv(cu_scr, t, cw_ref)
    if tail_at is not None:
        tail_ref[...] = cu_scr[tail_at:tail_at + HALO, :]
    cu_scr[0:HALO, :] = cu_scr[t:t + HALO, :]

    f = _dot(h, wf_ref[...]) + bf_ref[...]
    log_f = (jnp.minimum(f, 0.0) - jnp.log1p(jnp.exp(-jnp.abs(f)))) * LOG2E
    c, carry[...] = _cumsum_rows(log_f, carry[...])
    if tail_at is not None:
        c_ref[...] = c
    for head in range(N_HEADS):
        lo = head * 2 * HEAD_DIM + HEAD_DIM
        k_cols, q_cols = _bias_columns(c, head)
        kf_ref[:, lo:lo + HEAD_DIM] = k_cols.astype(BF16)
        qft_ref[lo:lo + HEAD_DIM, :] = q_cols.T.astype(BF16)

    vt_ref[...] = _dot(h, wqkv_ref[:, 2 * D_MODEL:3 * D_MODEL]).T.astype(BF16)
    q = _dot(h, wqkv_ref[:, 0:D_MODEL]) * Q_SCALE
    k = _dot(h, wqkv_ref[:, D_MODEL:2 * D_MODEL]).astype(BF16)
    for head in range(N_HEADS):
        src = slice(head * HEAD_DIM, (head + 1) * HEAD_DIM)
        dst = slice(head * 2 * HEAD_DIM, head * 2 * HEAD_DIM + HEAD_DIM)
        qft_ref[dst, :] = q[:, src].T.astype(BF16)
        kf_ref[:, dst] = k[:, src]
    yc_ref[...] = (_dot(h, wcv_ref[:, 0:D_MODEL]) * conv).astype(BF16)


def _mixer_in(x, g_mix, w_in, w_f, b_f, conv_w, halo, c0, *, tile, tail_at=None):
    b, n, _ = x.shape
    rows = pl.BlockSpec((None, tile, D_MODEL), lambda bi, i: (bi, i, 0))
    out_shape = [jax.ShapeDtypeStruct((b, 2 * D_MODEL, n), BF16), jax.ShapeDtypeStruct((b, n, 2 * D_MODEL), BF16),
                 jax.ShapeDtypeStruct((b, D_MODEL, n), BF16), jax.ShapeDtypeStruct((b, n, D_MODEL), BF16)]
    out_specs = [pl.BlockSpec((None, 2 * D_MODEL, tile), lambda bi, i: (bi, 0, i)),
                 pl.BlockSpec((None, tile, 2 * D_MODEL), lambda bi, i: (bi, i, 0)),
                 pl.BlockSpec((None, D_MODEL, tile), lambda bi, i: (bi, 0, i)), rows]
    if tail_at is not None:
        assert b == 1 and n == tile
        out_shape += [jax.ShapeDtypeStruct((b, n, LANES), F32), jax.ShapeDtypeStruct((HALO, D_MODEL), F32)]
        out_specs += [pl.BlockSpec((None, tile, LANES), lambda bi, i: (bi, i, 0)),
                      pl.BlockSpec((HALO, D_MODEL), lambda bi, i: (0, 0))]
    return pl.pallas_call(
        functools.partial(_mixer_in_kernel, tail_at=tail_at),
        grid=(b, n // tile),
        in_specs=[rows, _resident((1, D_MODEL)), _resident(*W_IN_BLOCKS["qkv"]), _resident(w_f.shape),
                  _resident(b_f.shape), _resident(*W_IN_BLOCKS["conv"]), _resident(conv_w.shape),
                  _resident(halo.shape), _resident(c0.shape)],
        out_specs=out_specs,
        out_shape=out_shape,
        scratch_shapes=[pltpu.VMEM((tile + HALO, D_MODEL), F32), pltpu.VMEM((1, LANES), F32)],
        compiler_params=pltpu.CompilerParams(dimension_semantics=("arbitrary", "arbitrary"),
                                             vmem_limit_bytes=VMEM_LIMIT),
        name="mixer_in" if tail_at is None else "mixer_in_meta",
    )(x, g_mix, w_in, w_f, b_f, w_in, conv_w, halo, c0)


def _attention_kernel(qft_ref, *rest, has_prefix, multi):
    rest = list(rest)
    qftn_ref = rest.pop(0) if multi else None
    kf_ref, vt_ref = rest.pop(0), rest.pop(0)
    kpf_ref, vtp_ref = (rest.pop(0), rest.pop(0)) if has_prefix else (None, None)
    o_ref, diag_scr, ab_scr, abmax_scr, m_scr, acc_scr = rest
    tq = qft_ref.shape[1]
    n_q = kf_ref.shape[0] // tq
    group = range(vt_ref.shape[0] // HEAD_DIM)
    i = pl.program_id(2)

    def cols(g):
        return slice(g * HEAD_DIM, (g + 1) * HEAD_DIM)

    def wide(g):
        return slice(g * 2 * HEAD_DIM, (g + 1) * 2 * HEAD_DIM)

    def rows_at(block):
        if isinstance(block, int):
            return slice(block * tq, (block + 1) * tq)
        return pl.ds(pl.multiple_of(block * tq, tq), tq)

    def ones_rows(keys):
        first_row = lax.broadcasted_iota(jnp.int32, (BF16_ROWS, keys), 0) == 0
        return jnp.where(first_row, 1.0, 0.0).astype(BF16)

    key = lax.broadcasted_iota(jnp.int32, (tq, tq), 0)
    qry = lax.broadcasted_iota(jnp.int32, (tq, tq), 1)
    ones_tq = ones_rows(tq)

    def q_cur(g):
        return qft_ref[wide(g), :]

    def produce(dst, block, q, max_dst=None):
        for g in group:
            s = _dot(kf_ref[rows_at(block), wide(g)], q(g))
            dst[g] = s
            if max_dst is not None:
                max_dst[g] = jnp.max(s, axis=0, keepdims=True)

    def update(g, s, vt_ones, s_max):
        m_old = m_scr[g]
        m_new = jnp.maximum(m_old, s_max)
        alpha = jnp.exp2(m_old - m_new)
        p = jnp.exp2(s - m_new).astype(BF16)
        acc_scr[g] = alpha * acc_scr[g] + _dot(vt_ones, p)
        m_scr[g] = m_new

    def prefix_scores():
        return [_dot(kpf_ref[:, wide(g)], q_cur(g)) for g in group] if has_prefix else None

    def consume(src, block, max_src):
        for g in group:
            vt_ones = jnp.concatenate([vt_ref[cols(g), rows_at(block)], ones_tq], axis=0)
            update(g, src[g], vt_ones, max_src[g])

    def start(s_prefix, block, diag):
        for g in group:
            s = jnp.where(key <= qry, diag[g], MASKED)
            vt = vt_ref[cols(g), rows_at(block)]
            if has_prefix:
                s = jnp.concatenate([s, s_prefix[g][0:N_META]], axis=0)
                vt = jnp.concatenate([vt, vtp_ref[cols(g), :]], axis=1)
            s_max = jnp.max(s, axis=0, keepdims=True)
            p = jnp.exp2(s - s_max).astype(BF16)
            if has_prefix:
                p = jnp.concatenate([p, jnp.zeros((vt.shape[1] - p.shape[0], tq), BF16)], axis=0)
            acc_scr[g] = _dot(jnp.concatenate([vt, ones_rows(vt.shape[1])], axis=0), p)
            m_scr[g] = s_max

    @pl.when(i == 0)
    def _():
        produce(diag_scr.at[0], 0, q_cur)

    if not multi:
        start(prefix_scores(), 0, diag_scr.at[0])
    else:
        def q_nxt(g):
            return qftn_ref[wide(g), :]

        def step(j, src, dst):
            produce(ab_scr.at[dst], j + 1, q_cur, abmax_scr.at[dst])
            consume(ab_scr.at[src], j, abmax_scr.at[src])

        def unrolled(blk):
            s_prefix = prefix_scores()
            if blk == 0:
                produce(diag_scr.at[1], 1, q_nxt)
                start(s_prefix, 0, diag_scr.at[0])
                return
            produce(ab_scr.at[0], 0, q_cur, abmax_scr.at[0])
            start(s_prefix, blk, diag_scr.at[blk & 1])
            for j in range(blk - 1):
                step(j, j & 1, (j + 1) & 1)
            produce(diag_scr.at[(blk + 1) & 1], blk + 1, q_nxt)
            consume(ab_scr.at[(blk - 1) & 1], blk - 1, abmax_scr.at[(blk - 1) & 1])

        def looped():
            s_prefix = prefix_scores()
            produce(ab_scr.at[(i - 1) & 1], 0, q_cur, abmax_scr.at[(i - 1) & 1])
            start(s_prefix, i, diag_scr.at[i & 1])

            @pl.when((i & 1) == 0)
            def _():
                step(0, 1, 0)

            def pair(jj, carry):
                j = 1 - (i & 1) + 2 * jj
                step(j, 0, 1)
                step(j + 1, 1, 0)
                return carry

            lax.fori_loop(0, lax.shift_right_logical(i - 1, 1), pair, 0)
            produce(diag_scr.at[(i + 1) & 1], jnp.minimum(i + 1, n_q - 1), q_nxt)
            consume(ab_scr.at[0], i - 1, abmax_scr.at[0])

        assert 2 <= UNROLLED_BLOCKS < n_q
        lax.switch(jnp.minimum(i, UNROLLED_BLOCKS),
                   [functools.partial(unrolled, blk) for blk in range(UNROLLED_BLOCKS)] + [looped])

    for g in group:
        acc = acc_scr[g]
        inv_l = 1.0 / acc[HEAD_DIM:HEAD_DIM + 1]
        o_ref[:, cols(g)] = (acc[0:HEAD_DIM] * inv_l).T.astype(o_ref.dtype)


def _attention(qft, kf, vt, prefix, *, tq, heads):
    b, n, _ = kf.shape
    n_q = n // tq
    width = heads * HEAD_DIM
    in_specs, args = [pl.BlockSpec((None, 2 * width, tq), lambda bi, h, i: (bi, h, i))], [qft]
    if n_q > 1:
        in_specs.append(pl.BlockSpec((None, 2 * width, tq), lambda bi, h, i: (bi, h, jnp.minimum(i + 1, n_q - 1))))
        args.append(qft)
    in_specs += [pl.BlockSpec((None, n, 2 * width), lambda bi, h, i: (bi, 0, h)),
                 pl.BlockSpec((None, width, n), lambda bi, h, i: (bi, h, 0))]
    args += [kf, vt]
    if prefix is not None:
        kpf, vtp = prefix
        n_p = kpf.shape[0]
        in_specs += [pl.BlockSpec((n_p, 2 * width), lambda bi, h, i: (0, h)),
                     pl.BlockSpec((width, n_p), lambda bi, h, i: (h, 0))]
        args += [kpf, vtp]
    scratch = [pltpu.VMEM((2, heads, tq, tq), F32), pltpu.VMEM((2, heads, tq, tq), F32),
               pltpu.VMEM((2, heads, 1, tq), F32), pltpu.VMEM((heads, 1, tq), F32),
               pltpu.VMEM((heads, HEAD_DIM + BF16_ROWS, tq), F32)]
    return pl.pallas_call(
        functools.partial(_attention_kernel, has_prefix=prefix is not None, multi=n_q > 1),
        grid=(b, N_HEADS // heads, n_q),
        in_specs=in_specs,
        out_specs=pl.BlockSpec((None, tq, width), lambda bi, h, i: (bi, i, h)),
        out_shape=jax.ShapeDtypeStruct((b, n, D_MODEL), BF16),
        scratch_shapes=scratch,
        compiler_params=pltpu.CompilerParams(dimension_semantics=("arbitrary",) * 3,
                                             vmem_limit_bytes=VMEM_LIMIT),
        name="attention" if prefix is not None else "attention_meta",
    )(*args)


def _mixer_out_kernel(x_ref, att_ref, yc_ref, gmix_ref, gffn_ref, gfin_ref, wg_ref, woc_ref, woa_ref,
                      wo_ref, wa_ref, wv_ref, wd_ref, fcw_ref, halo_ref, out_ref, a_scr, *, tail_at):
    t = x_ref.shape[0]

    @pl.when(pl.program_id(1) == 0)
    def _():
        a_scr[0:HALO, :] = halo_ref[...]

    x = x_ref[...]
    h = _rms_norm(x, gmix_ref[...]).astype(BF16)
    g_att = _sigmoid(_dot(h, wg_ref[:, 0:D_MODEL]))
    g_conv = _sigmoid(_dot(h, wg_ref[:, D_MODEL:2 * D_MODEL]))
    mix = g_att * _dot(att_ref[...], woa_ref[...]) + g_conv * _dot(yc_ref[...], woc_ref[...])
    z1 = x + _dot(mix.astype(BF16), wo_ref[...])
    h2 = _rms_norm(z1, gffn_ref[...]).astype(BF16)

    if tail_at is not None:
        a_scr[HALO:HALO + t, :] = _dot(h2, wa_ref[...])
        out_ref[...] = a_scr[tail_at:tail_at + HALO, :]
        return

    a_scr[HALO:HALO + t, :] = _dot(h2, wa_ref[...])
    conv = _causal_conv(a_scr, t, fcw_ref)
    act = conv * _sigmoid(conv) * _dot(h2, wv_ref[...])
    a_scr[0:HALO, :] = a_scr[t:t + HALO, :]
    out_ref[...] = _rms_norm(z1 + _dot(act.astype(BF16), wd_ref[...]), gfin_ref[...])


def _mixer_out(x, att, yc, g_mix, g_ffn, g_final, w_in, w_sq, w_ffn, w_d, ffn_conv_w, halo, *, tile, tail_at=None):
    b, n, _ = x.shape
    square = [_resident((None, D_MODEL, D_MODEL), (k, 0, 0)) for k in range(3)]
    halves = [_resident((D_MODEL, D_FF), (0, k)) for k in range(2)]
    rows = pl.BlockSpec((None, tile, D_MODEL), lambda bi, i: (bi, i, 0))
    if tail_at is not None:
        assert b == 1 and n == tile
        out_shape = jax.ShapeDtypeStruct((HALO, D_FF), F32)
        out_spec = pl.BlockSpec((HALO, D_FF), lambda bi, i: (0, 0))
    else:
        out_shape = jax.ShapeDtypeStruct(x.shape, F32)
        out_spec = rows
    consts = (g_mix, g_ffn, g_final, w_in, w_sq, w_sq, w_sq, w_ffn, w_ffn, w_d, ffn_conv_w, halo)
    const_specs = ([_resident(g_mix.shape)] * 3 + [_resident(*W_IN_BLOCKS["gates"])] + square + halves
                   + [_resident(w_d.shape), _resident(ffn_conv_w.shape), _resident(halo.shape)])
    return pl.pallas_call(
        functools.partial(_mixer_out_kernel, tail_at=tail_at),
        grid=(b, n // tile),
        in_specs=[rows, rows, rows] + const_specs,
        out_specs=out_spec,
        out_shape=out_shape,
        scratch_shapes=[pltpu.VMEM((tile + HALO, D_FF), F32)],
        compiler_params=pltpu.CompilerParams(dimension_semantics=("arbitrary", "arbitrary"),
                                             vmem_limit_bytes=VMEM_LIMIT),
        name="mixer_out" if tail_at is None else "mixer_out_meta",
    )(x, att, yc, *consts)


def kernel(x, meta_tokens, g_mix, w_in, b_f, conv_w, w_o_attn, w_o_conv, w_o, g_ffn, w_ffn_in, ffn_conv_w,
           w_ffn_out, g_final):
    assert w_in.shape[0] == 1, "one layer"
    d = D_MODEL
    w_in_packed, w_f = _repack_w_in(w_in[0].T)
    b_f_row = jnp.pad(b_f[0], (0, LANES - N_HEADS))[None, :]
    w_sq = jnp.stack([w_o_conv[0], w_o_attn[0], w_o[0]]).astype(BF16)
    w_ffn = w_ffn_in[0].astype(BF16)
    w_d = w_ffn_out[0].astype(BF16)
    g_mix_row, g_ffn_row, g_final_row = g_mix[0][None, :], g_ffn[0][None, :], g_final[None, :]
    in_consts = (g_mix_row, w_in_packed, w_f, b_f_row, conv_w[0])
    out_consts = (g_mix_row, g_ffn_row, g_final_row, w_in_packed, w_sq, w_ffn, w_d, ffn_conv_w[0])

    meta = jnp.pad(meta_tokens.astype(F32), ((0, META_TILE - N_META), (0, 0)))[None]
    qftm, kfm, vtm, ycm, cm, cu_tail = _mixer_in(
        meta, *in_consts, jnp.zeros((HALO, d), F32), jnp.zeros((1, LANES), F32), tile=META_TILE, tail_at=N_META)
    att_m = _attention(qftm, kfm, vtm, None, tq=META_TILE, heads=HEADS_PER_STEP)
    a_tail = _mixer_out(meta, att_m, ycm, *out_consts, jnp.zeros((HALO, D_FF), F32),
                        tile=META_TILE, tail_at=N_META)

    qft, kf, vt, yc = _mixer_in(x, *in_consts, cu_tail, cm[0, N_META - 1:N_META, :], tile=ROW_TILE_IN)
    att = _attention(qft, kf, vt, (kfm[0], vtm[0]), tq=Q_TILE, heads=HEADS_PER_STEP)
    return _mixer_out(x, att, yc, *out_consts, a_tail, tile=ROW_TILE_OUT)
```

```python
import functools

import jax
import jax.numpy as jnp
from jax import lax
from jax.experimental import pallas as pl
from jax.experimental.pallas import tpu as pltpu

D_MODEL = 1024
N_HEADS = 8
HEAD_DIM = 128
N_META = 16
D_FF = 2816
RMS_EPS = 1e-6

LANES = 128
SUBLANES = 8
BF16_ROWS = 2 * SUBLANES
HALO = SUBLANES
VMEM_LIMIT = 56 * 1024 * 1024

LOG2E = 1.4426950408889634
Q_SCALE = HEAD_DIM ** -0.5 * LOG2E
MASKED = -1e30

META_TILE = LANES
ROW_TILE_IN = 512
ROW_TILE_OUT = 512
Q_TILE = 512
HEADS_PER_STEP = 4
UNROLLED_BLOCKS = 4
W_IN_BLOCKS = {
    "qkv": ((D_MODEL, 3 * D_MODEL), (0, 0)),
    "conv": ((D_MODEL, 3 * D_MODEL), (0, 1)),
    "gates": ((D_MODEL, 2 * D_MODEL), (0, 3)),
}

BF16 = jnp.bfloat16
F32 = jnp.float32


def _dot(a, b):
    return jnp.dot(a, b, preferred_element_type=F32)


def _rms_norm(x, g):
    return x * lax.rsqrt(jnp.mean(x * x, axis=-1, keepdims=True) + RMS_EPS) * g


def _sigmoid(x):
    return 1.0 / (1.0 + jnp.exp(-x))


def _split3(x):
    hi = x.astype(BF16).astype(F32)
    mid = (x - hi).astype(BF16).astype(F32)
    lo = (x - hi - mid).astype(BF16).astype(F32)
    return hi, mid, lo


def _cumsum_rows(x, start):
    t = x.shape[0]
    row = lax.broadcasted_iota(jnp.int32, (LANES, LANES), 0)
    col = lax.broadcasted_iota(jnp.int32, (LANES, LANES), 1)
    tri = jnp.where(col <= row, 1.0, 0.0).astype(BF16)
    blocks = []
    for r in range(0, t, LANES):
        hi, mid, lo = _split3(x[r:r + LANES])
        local = _dot(tri, hi.astype(BF16)) + _dot(tri, mid.astype(BF16)) + _dot(tri, lo.astype(BF16))
        blocks.append(local + start)
        start = blocks[-1][LANES - 1:LANES]
    return jnp.concatenate(blocks, axis=0), start


def _causal_conv(scr, t, w_ref):
    out = scr[HALO - 2:HALO - 2 + t, :] * w_ref[0:1, :]
    out = out + scr[HALO - 1:HALO - 1 + t, :] * w_ref[1:2, :]
    return out + scr[HALO:HALO + t, :] * w_ref[2:3, :]


def _resident(block_shape, block_index=None):
    index = tuple(block_index) if block_index is not None else (0,) * len(block_shape)
    return pl.BlockSpec(block_shape, lambda *_: index, pipeline_mode=pl.Buffered(1))


def _repack_w_in_kernel(w_ref, f_ref, o_ref, of_ref):
    o_ref[...] = w_ref[...].T.astype(BF16)
    f_rows = jnp.concatenate([f_ref[...], jnp.zeros((LANES - N_HEADS, D_MODEL), F32)], axis=0)
    of_ref[...] = f_rows.T.astype(BF16)


def _repack_w_in(w_t):
    d = D_MODEL
    assert N_HEADS % SUBLANES == 0
    return pl.pallas_call(
        _repack_w_in_kernel,
        grid=(8,),
        in_specs=[pl.BlockSpec((pl.Element(d), pl.Element(d)),
                               lambda j: ((j * (d // N_HEADS) + jnp.where(j >= 3, 1, 0)) * N_HEADS, 0)),
                  pl.BlockSpec((N_HEADS, d), lambda j: (3 * d // N_HEADS, 0))],
        out_specs=[pl.BlockSpec((d, d), lambda j: (0, j)), pl.BlockSpec((d, LANES), lambda j: (0, 0))],
        out_shape=[jax.ShapeDtypeStruct((d, 8 * d), BF16), jax.ShapeDtypeStruct((d, LANES), BF16)],
        compiler_params=pltpu.CompilerParams(dimension_semantics=("arbitrary",), vmem_limit_bytes=VMEM_LIMIT),
        name="repack_w_in",
    )(w_t, w_t)


def _bias_columns(c, head):
    lane = lax.broadcasted_iota(jnp.int32, c.shape, 1)
    pieces = _split3(c[:, head:head + 1])
    k_side = jnp.where((lane >= 3) & (lane < 6), 1.0, 0.0)
    q_side = jnp.where(lane < 3, 1.0, 0.0)
    for n, piece in enumerate(pieces):
        k_side = jnp.where(lane == n, -piece, k_side)
        q_side = jnp.where(lane == 3 + n, piece, q_side)
    return k_side, q_side


def _mixer_in_kernel(x_ref, g_ref, wqkv_ref, wf_ref, bf_ref, wcv_ref, cw_ref, halo_ref, c0_ref,
                     qft_ref, kf_ref, vt_ref, yc_ref, *rest, tail_at):
    if tail_at is not None:
        c_ref, tail_ref, cu_scr, carry = rest
    else:
        cu_scr, carry = rest
    t = x_ref.shape[0]

    @pl.when(pl.program_id(1) == 0)
    def _():
        cu_scr[0:HALO, :] = halo_ref[...]
        carry[...] = c0_ref[...]

    h = _rms_norm(x_ref[...], g_ref[...]).astype(BF16)

    gc = _dot(h, wcv_ref[:, D_MODEL:2 * D_MODEL])
    u = _dot(h, wcv_ref[:, 2 * D_MODEL:3 * D_MODEL])
    cu_scr[HALO:HALO + t, :] = gc * u
    conv = _causal_conv(cu_scr, t, cw_ref)
    if tail_at is not None:
        tail_ref[...] = cu_scr[tail_at:tail_at + HALO, :]
    cu_scr[0:HALO, :] = cu_scr[t:t + HALO, :]

    f = _dot(h, wf_ref[...]) + bf_ref[...]
    log_f = (jnp.minimum(f, 0.0) - jnp.log1p(jnp.exp(-jnp.abs(f)))) * LOG2E
    c, carry[...] = _cumsum_rows(log_f, carry[...])
    if tail_at is not None:
        c_ref[...] = c
    for head in range(N_HEADS):
        lo = head * 2 * HEAD_DIM + HEAD_DIM
        k_cols, q_cols = _bias_columns(c, head)
        kf_ref[:, lo:lo + HEAD_DIM] = k_cols.astype(BF16)
        qft_ref[lo:lo + HEAD_DIM, :] = q_cols.T.astype(BF16)

    vt_ref[...] = _dot(h, wqkv_ref[:, 2 * D_MODEL:3 * D_MODEL]).T.astype(BF16)
    q = _dot(h, wqkv_ref[:, 0:D_MODEL]) * Q_SCALE
    k = _dot(h, wqkv_ref[:, D_MODEL:2 * D_MODEL]).astype(BF16)
    for head in range(N_HEADS):
        src = slice(head * HEAD_DIM, (head + 1) * HEAD_DIM)
        dst = slice(head * 2 * HEAD_DIM, head * 2 * HEAD_DIM + HEAD_DIM)
        qft_ref[dst, :] = q[:, src].T.astype(BF16)
        kf_ref[:, dst] = k[:, src]
    yc_ref[...] = (_dot(h, wcv_ref[:, 0:D_MODEL]) * conv).astype(BF16)


def _mixer_in(x, g_mix, w_in, w_f, b_f, conv_w, halo, c0, *, tile, tail_at=None):
    b, n, _ = x.shape
    rows = pl.BlockSpec((None, tile, D_MODEL), lambda bi, i: (bi, i, 0))
    out_shape = [jax.ShapeDtypeStruct((b, 2 * D_MODEL, n), BF16), jax.ShapeDtypeStruct((b, n, 2 * D_MODEL), BF16),
                 jax.ShapeDtypeStruct((b, D_MODEL, n), BF16), jax.ShapeDtypeStruct((b, n, D_MODEL), BF16)]
    out_specs = [pl.BlockSpec((None, 2 * D_MODEL, tile), lambda bi, i: (bi, 0, i)),
                 pl.BlockSpec((None, tile, 2 * D_MODEL), lambda bi, i: (bi, i, 0)),
                 pl.BlockSpec((None, D_MODEL, tile), lambda bi, i: (bi, 0, i)), rows]
    if tail_at is not None:
        assert b == 1 and n == tile
        out_shape += [jax.ShapeDtypeStruct((b, n, LANES), F32), jax.ShapeDtypeStruct((HALO, D_MODEL), F32)]
        out_specs += [pl.BlockSpec((None, tile, LANES), lambda bi, i: (bi, i, 0)),
                      pl.BlockSpec((HALO, D_MODEL), lambda bi, i: (0, 0))]
    return pl.pallas_call(
        functools.partial(_mixer_in_kernel, tail_at=tail_at),
        grid=(b, n // tile),
        in_specs=[rows, _resident((1, D_MODEL)), _resident(*W_IN_BLOCKS["qkv"]), _resident(w_f.shape),
                  _resident(b_f.shape), _resident(*W_IN_BLOCKS["conv"]), _resident(conv_w.shape),
                  _resident(halo.shape), _resident(c0.shape)],
        out_specs=out_specs,
        out_shape=out_shape,
        scratch_shapes=[pltpu.VMEM((tile + HALO, D_MODEL), F32), pltpu.VMEM((1, LANES), F32)],
        compiler_params=pltpu.CompilerParams(dimension_semantics=("arbitrary", "arbitrary"),
                                             vmem_limit_bytes=VMEM_LIMIT),
        name="mixer_in" if tail_at is None else "mixer_in_meta",
    )(x, g_mix, w_in, w_f, b_f, w_in, conv_w, halo, c0)


def _attention_kernel(qft_ref, *rest, has_prefix, multi):
    rest = list(rest)
    qftn_ref = rest.pop(0) if multi else None
    kf_ref, vt_ref = rest.pop(0), rest.pop(0)
    kpf_ref, vtp_ref = (rest.pop(0), rest.pop(0)) if has_prefix else (None, None)
    o_ref, diag_scr, ab_scr, abmax_scr, m_scr, acc_scr = rest
    tq = qft_ref.shape[1]
    n_q = kf_ref.shape[0] // tq
    group = range(vt_ref.shape[0] // HEAD_DIM)
    i = pl.program_id(2)

    def cols(g):
        return slice(g * HEAD_DIM, (g + 1) * HEAD_DIM)

    def wide(g):
        return slice(g * 2 * HEAD_DIM, (g + 1) * 2 * HEAD_DIM)

    def rows_at(block):
        if isinstance(block, int):
            return slice(block * tq, (block + 1) * tq)
        return pl.ds(pl.multiple_of(block * tq, tq), tq)

    def ones_rows(keys):
        first_row = lax.broadcasted_iota(jnp.int32, (BF16_ROWS, keys), 0) == 0
        return jnp.where(first_row, 1.0, 0.0).astype(BF16)

    key = lax.broadcasted_iota(jnp.int32, (tq, tq), 0)
    qry = lax.broadcasted_iota(jnp.int32, (tq, tq), 1)
    ones_tq = ones_rows(tq)

    def q_cur(g):
        return qft_ref[wide(g), :]

    def produce(dst, block, q, max_dst=None):
        for g in group:
            s = _dot(kf_ref[rows_at(block), wide(g)], q(g))
            dst[g] = s
            if max_dst is not None:
                max_dst[g] = jnp.max(s, axis=0, keepdims=True)

    def update(g, s, vt_ones, s_max):
        m_old = m_scr[g]
        m_new = jnp.maximum(m_old, s_max)
        alpha = jnp.exp2(m_old - m_new)
        p = jnp.exp2(s - m_new).astype(BF16)
        acc_scr[g] = alpha * acc_scr[g] + _dot(vt_ones, p)
        m_scr[g] = m_new

    def prefix_scores():
        return [_dot(kpf_ref[:, wide(g)], q_cur(g)) for g in group] if has_prefix else None

    def consume(src, block, max_src):
        for g in group:
            vt_ones = jnp.concatenate([vt_ref[cols(g), rows_at(block)], ones_tq], axis=0)
            update(g, src[g], vt_ones, max_src[g])

    def start(s_prefix, block, diag):
        for g in group:
            s = jnp.where(key <= qry, diag[g], MASKED)
            vt = vt_ref[cols(g), rows_at(block)]
            if has_prefix:
                s = jnp.concatenate([s, s_prefix[g][0:N_META]], axis=0)
                vt = jnp.concatenate([vt, vtp_ref[cols(g), :]], axis=1)
            s_max = jnp.max(s, axis=0, keepdims=True)
            p = jnp.exp2(s - s_max).astype(BF16)
            if has_prefix:
                p = jnp.concatenate([p, jnp.zeros((vt.shape[1] - p.shape[0], tq), BF16)], axis=0)
            acc_scr[g] = _dot(jnp.concatenate([vt, ones_rows(vt.shape[1])], axis=0), p)
            m_scr[g] = s_max

    @pl.when(i == 0)
    def _():
        produce(diag_scr.at[0], 0, q_cur)

    if not multi:
        start(prefix_scores(), 0, diag_scr.at[0])
    else:
        def q_nxt(g):
            return qftn_ref[wide(g), :]

        def step(j, src, dst):
            produce(ab_scr.at[dst], j + 1, q_cur, abmax_scr.at[dst])
            consume(ab_scr.at[src], j, abmax_scr.at[src])

        def unrolled(blk):
            s_prefix = prefix_scores()
            if blk == 0:
                produce(diag_scr.at[1], 1, q_nxt)
                start(s_prefix, 0, diag_scr.at[0])
                return
            produce(ab_scr.at[0], 0, q_cur, abmax_scr.at[0])
            start(s_prefix, blk, diag_scr.at[blk & 1])
            for j in range(blk - 1):
                step(j, j & 1, (j + 1) & 1)
            produce(diag_scr.at[(blk + 1) & 1], blk + 1, q_nxt)
            consume(ab_scr.at[(blk - 1) & 1], blk - 1, abmax_scr.at[(blk - 1) & 1])

        def looped():
            s_prefix = prefix_scores()
            produce(ab_scr.at[(i - 1) & 1], 0, q_cur, abmax_scr.at[(i - 1) & 1])
            start(s_prefix, i, diag_scr.at[i & 1])

            @pl.when((i & 1) == 0)
            def _():
                step(0, 1, 0)

            def pair(jj, carry):
                j = 1 - (i & 1) + 2 * jj
                step(j, 0, 1)
                step(j + 1, 1, 0)
                return carry

            lax.fori_loop(0, lax.shift_right_logical(i - 1, 1), pair, 0)
            produce(diag_scr.at[(i + 1) & 1], jnp.minimum(i + 1, n_q - 1), q_nxt)
            consume(ab_scr.at[0], i - 1, abmax_scr.at[0])

        assert 2 <= UNROLLED_BLOCKS < n_q
        lax.switch(jnp.minimum(i, UNROLLED_BLOCKS),
                   [functools.partial(unrolled, blk) for blk in range(UNROLLED_BLOCKS)] + [looped])

    for g in group:
        acc = acc_scr[g]
        inv_l = 1.0 / acc[HEAD_DIM:HEAD_DIM + 1]
        o_ref[:, cols(g)] = (acc[0:HEAD_DIM] * inv_l).T.astype(o_ref.dtype)


def _attention(qft, kf, vt, prefix, *, tq, heads):
    b, n, _ = kf.shape
    n_q = n // tq
    width = heads * HEAD_DIM
    in_specs, args = [pl.BlockSpec((None, 2 * width, tq), lambda bi, h, i: (bi, h, i))], [qft]
    if n_q > 1:
        in_specs.append(pl.BlockSpec((None, 2 * width, tq), lambda bi, h, i: (bi, h, jnp.minimum(i + 1, n_q - 1))))
        args.append(qft)
    in_specs += [pl.BlockSpec((None, n, 2 * width), lambda bi, h, i: (bi, 0, h)),
                 pl.BlockSpec((None, width, n), lambda bi, h, i: (bi, h, 0))]
    args += [kf, vt]
    if prefix is not None:
        kpf, vtp = prefix
        n_p = kpf.shape[0]
        in_specs += [pl.BlockSpec((n_p, 2 * width), lambda bi, h, i: (0, h)),
                     pl.BlockSpec((width, n_p), lambda bi, h, i: (h, 0))]
        args += [kpf, vtp]
    scratch = [pltpu.VMEM((2, heads, tq, tq), F32), pltpu.VMEM((2, heads, tq, tq), F32),
               pltpu.VMEM((2, heads, 1, tq), F32), pltpu.VMEM((heads, 1, tq), F32),
               pltpu.VMEM((heads, HEAD_DIM + BF16_ROWS, tq), F32)]
    return pl.pallas_call(
        functools.partial(_attention_kernel, has_prefix=prefix is not None, multi=n_q > 1),
        grid=(b, N_HEADS // heads, n_q),
        in_specs=in_specs,
        out_specs=pl.BlockSpec((None, tq, width), lambda bi, h, i: (bi, i, h)),
        out_shape=jax.ShapeDtypeStruct((b, n, D_MODEL), BF16),
        scratch_shapes=scratch,
        compiler_params=pltpu.CompilerParams(dimension_semantics=("arbitrary",) * 3,
                                             vmem_limit_bytes=VMEM_LIMIT),
        name="attention" if prefix is not None else "attention_meta",
    )(*args)


def _mixer_out_kernel(x_ref, att_ref, yc_ref, gmix_ref, gffn_ref, gfin_ref, wg_ref, woc_ref, woa_ref,
                      wo_ref, wa_ref, wv_ref, wd_ref, fcw_ref, halo_ref, out_ref, a_scr, *, tail_at):
    t = x_ref.shape[0]

    @pl.when(pl.program_id(1) == 0)
    def _():
        a_scr[0:HALO, :] = halo_ref[...]

    x = x_ref[...]
    h = _rms_norm(x, gmix_ref[...]).astype(BF16)
    g_att = _sigmoid(_dot(h, wg_ref[:, 0:D_MODEL]))
    g_conv = _sigmoid(_dot(h, wg_ref[:, D_MODEL:2 * D_MODEL]))
    mix = g_att * _dot(att_ref[...], woa_ref[...]) + g_conv * _dot(yc_ref[...], woc_ref[...])
    z1 = x + _dot(mix.astype(BF16), wo_ref[...])
    h2 = _rms_norm(z1, gffn_ref[...]).astype(BF16)

    if tail_at is not None:
        a_scr[HALO:HALO + t, :] = _dot(h2, wa_ref[...])
        out_ref[...] = a_scr[tail_at:tail_at + HALO, :]
        return

    a_scr[HALO:HALO + t, :] = _dot(h2, wa_ref[...])
    conv = _causal_conv(a_scr, t, fcw_ref)
    act = conv * _sigmoid(conv) * _dot(h2, wv_ref[...])
    a_scr[0:HALO, :] = a_scr[t:t + HALO, :]
    out_ref[...] = _rms_norm(z1 + _dot(act.astype(BF16), wd_ref[...]), gfin_ref[...])


def _mixer_out(x, att, yc, g_mix, g_ffn, g_final, w_in, w_sq, w_ffn, w_d, ffn_conv_w, halo, *, tile, tail_at=None):
    b, n, _ = x.shape
    square = [_resident((None, D_MODEL, D_MODEL), (k, 0, 0)) for k in range(3)]
    halves = [_resident((D_MODEL, D_FF), (0, k)) for k in range(2)]
    rows = pl.BlockSpec((None, tile, D_MODEL), lambda bi, i: (bi, i, 0))
    if tail_at is not None:
        assert b == 1 and n == tile
        out_shape = jax.ShapeDtypeStruct((HALO, D_FF), F32)
        out_spec = pl.BlockSpec((HALO, D_FF), lambda bi, i: (0, 0))
    else:
        out_shape = jax.ShapeDtypeStruct(x.shape, F32)
        out_spec = rows
    consts = (g_mix, g_ffn, g_final, w_in, w_sq, w_sq, w_sq, w_ffn, w_ffn, w_d, ffn_conv_w, halo)
    const_specs = ([_resident(g_mix.shape)] * 3 + [_resident(*W_IN_BLOCKS["gates"])] + square + halves
                   + [_resident(w_d.shape), _resident(ffn_conv_w.shape), _resident(halo.shape)])
    return pl.pallas_call(
        functools.partial(_mixer_out_kernel, tail_at=tail_at),
        grid=(b, n // tile),
        in_specs=[rows, rows, rows] + const_specs,
        out_specs=out_spec,
        out_shape=out_shape,
        scratch_shapes=[pltpu.VMEM((tile + HALO, D_FF), F32)],
        compiler_params=pltpu.CompilerParams(dimension_semantics=("arbitrary", "arbitrary"),
                                             vmem_limit_bytes=VMEM_LIMIT),
        name="mixer_out" if tail_at is None else "mixer_out_meta",
    )(x, att, yc, *consts)


def kernel(x, meta_tokens, g_mix, w_in, b_f, conv_w, w_o_attn, w_o_conv, w_o, g_ffn, w_ffn_in, ffn_conv_w,
           w_ffn_out, g_final):
    assert w_in.shape[0] == 1, "one layer"
    d = D_MODEL
    w_in_packed, w_f = _repack_w_in(w_in[0].T)
    b_f_row = jnp.pad(b_f[0], (0, LANES - N_HEADS))[None, :]
    w_sq = jnp.stack([w_o_conv[0], w_o_attn[0], w_o[0]]).astype(BF16)
    w_ffn = w_ffn_in[0].astype(BF16)
    w_d = w_ffn_out[0].astype(BF16)
    g_mix_row, g_ffn_row, g_final_row = g_mix[0][None, :], g_ffn[0][None, :], g_final[None, :]
    in_consts = (g_mix_row, w_in_packed, w_f, b_f_row, conv_w[0])
    out_consts = (g_mix_row, g_ffn_row, g_final_row, w_in_packed, w_sq, w_ffn, w_d, ffn_conv_w[0])

    meta = jnp.pad(meta_tokens.astype(F32), ((0, META_TILE - N_META), (0, 0)))[None]
    qftm, kfm, vtm, ycm, cm, cu_tail = _mixer_in(
        meta, *in_consts, jnp.zeros((HALO, d), F32), jnp.zeros((1, LANES), F32), tile=META_TILE, tail_at=N_META)
    att_m = _attention(qftm, kfm, vtm, None, tq=META_TILE, heads=HEADS_PER_STEP)
    a_tail = _mixer_out(meta, att_m, ycm, *out_consts, jnp.zeros((HALO, D_FF), F32),
                        tile=META_TILE, tail_at=N_META)

    qft, kf, vt, yc = _mixer_in(x, *in_consts, cu_tail, cm[0, N_META - 1:N_META, :], tile=ROW_TILE_IN)
    att = _attention(qft, kf, vt, (kfm[0], vtm[0]), tq=Q_TILE, heads=HEADS_PER_STEP)
    return _mixer_out(x, att, yc, *out_consts, a_tail, tile=ROW_TILE_OUT)
```

```python
import functools

import jax
import jax.numpy as jnp
from jax import lax
from jax.experimental import pallas as pl
from jax.experimental.pallas import tpu as pltpu

D_MODEL = 1024
N_HEADS = 8
HEAD_DIM = 128
N_META = 16
D_FF = 2816
RMS_EPS = 1e-6

LANES = 128
SUBLANES = 8
BF16_ROWS = 2 * SUBLANES
HALO = SUBLANES
VMEM_LIMIT = 56 * 1024 * 1024

LOG2E = 1.4426950408889634
Q_SCALE = HEAD_DIM ** -0.5 * LOG2E
MASKED = -1e30

META_TILE = LANES
ROW_TILE_IN = 512
ROW_TILE_OUT = 512
Q_TILE = 512
HEADS_PER_STEP = 4
UNROLLED_BLOCKS = 5
W_IN_BLOCKS = {
    "qkv": ((D_MODEL, 3 * D_MODEL), (0, 0)),
    "conv": ((D_MODEL, 3 * D_MODEL), (0, 1)),
    "gates": ((D_MODEL, 2 * D_MODEL), (0, 3)),
}

BF16 = jnp.bfloat16
F32 = jnp.float32


def _dot(a, b):
    return jnp.dot(a, b, preferred_element_type=F32)


def _rms_norm(x, g):
    return x * lax.rsqrt(jnp.mean(x * x, axis=-1, keepdims=True) + RMS_EPS) * g


def _sigmoid(x):
    return 1.0 / (1.0 + jnp.exp(-x))


def _split3(x):
    hi = x.astype(BF16).astype(F32)
    mid = (x - hi).astype(BF16).astype(F32)
    lo = (x - hi - mid).astype(BF16).astype(F32)
    return hi, mid, lo


def _cumsum_rows(x, start):
    t = x.shape[0]
    row = lax.broadcasted_iota(jnp.int32, (LANES, LANES), 0)
    col = lax.broadcasted_iota(jnp.int32, (LANES, LANES), 1)
    tri = jnp.where(col <= row, 1.0, 0.0).astype(BF16)
    blocks = []
    for r in range(0, t, LANES):
        hi, mid, lo = _split3(x[r:r + LANES])
        local = _dot(tri, hi.astype(BF16)) + _dot(tri, mid.astype(BF16)) + _dot(tri, lo.astype(BF16))
        blocks.append(local + start)
        start = blocks[-1][LANES - 1:LANES]
    return jnp.concatenate(blocks, axis=0), start


def _causal_conv(scr, t, w_ref):
    out = scr[HALO - 2:HALO - 2 + t, :] * w_ref[0:1, :]
    out = out + scr[HALO - 1:HALO - 1 + t, :] * w_ref[1:2, :]
    return out + scr[HALO:HALO + t, :] * w_ref[2:3, :]


def _resident(block_shape, block_index=None):
    index = tuple(block_index) if block_index is not None else (0,) * len(block_shape)
    return pl.BlockSpec(block_shape, lambda *_: index, pipeline_mode=pl.Buffered(1))


def _repack_w_in_kernel(w_ref, f_ref, o_ref, of_ref):
    o_ref[...] = w_ref[...].T.astype(BF16)
    f_rows = jnp.concatenate([f_ref[...], jnp.zeros((LANES - N_HEADS, D_MODEL), F32)], axis=0)
    of_ref[...] = f_rows.T.astype(BF16)


def _repack_w_in(w_t):
    d = D_MODEL
    assert N_HEADS % SUBLANES == 0
    return pl.pallas_call(
        _repack_w_in_kernel,
        grid=(8,),
        in_specs=[pl.BlockSpec((pl.Element(d), pl.Element(d)),
                               lambda j: ((j * (d // N_HEADS) + jnp.where(j >= 3, 1, 0)) * N_HEADS, 0)),
                  pl.BlockSpec((N_HEADS, d), lambda j: (3 * d // N_HEADS, 0))],
        out_specs=[pl.BlockSpec((d, d), lambda j: (0, j)), pl.BlockSpec((d, LANES), lambda j: (0, 0))],
        out_shape=[jax.ShapeDtypeStruct((d, 8 * d), BF16), jax.ShapeDtypeStruct((d, LANES), BF16)],
        compiler_params=pltpu.CompilerParams(dimension_semantics=("arbitrary",), vmem_limit_bytes=VMEM_LIMIT),
        name="repack_w_in",
    )(w_t, w_t)


def _bias_columns(c, head):
    lane = lax.broadcasted_iota(jnp.int32, c.shape, 1)
    pieces = _split3(c[:, head:head + 1])
    k_side = jnp.where((lane >= 3) & (lane < 6), 1.0, 0.0)
    q_side = jnp.where(lane < 3, 1.0, 0.0)
    for n, piece in enumerate(pieces):
        k_side = jnp.where(lane == n, -piece, k_side)
        q_side = jnp.where(lane == 3 + n, piece, q_side)
    return k_side, q_side


def _mixer_in_kernel(x_ref, g_ref, wqkv_ref, wf_ref, bf_ref, wcv_ref, cw_ref, halo_ref, c0_ref,
                     qft_ref, kf_ref, vt_ref, yc_ref, *rest, tail_at):
    if tail_at is not None:
        c_ref, tail_ref, cu_scr, carry = rest
    else:
        cu_scr, carry = rest
    t = x_ref.shape[0]

    @pl.when(pl.program_id(1) == 0)
    def _():
        cu_scr[0:HALO, :] = halo_ref[...]
        carry[...] = c0_ref[...]

    h = _rms_norm(x_ref[...], g_ref[...]).astype(BF16)

    gc = _dot(h, wcv_ref[:, D_MODEL:2 * D_MODEL])
    u = _dot(h, wcv_ref[:, 2 * D_MODEL:3 * D_MODEL])
    cu_scr[HALO:HALO + t, :] = gc * u
    conv = _causal_conv(cu_scr, t, cw_ref)
    if tail_at is not None:
        tail_ref[...] = cu_scr[tail_at:tail_at + HALO, :]
    cu_scr[0:HALO, :] = cu_scr[t:t + HALO, :]

    f = _dot(h, wf_ref[...]) + bf_ref[...]
    log_f = (jnp.minimum(f, 0.0) - jnp.log1p(jnp.exp(-jnp.abs(f)))) * LOG2E
    c, carry[...] = _cumsum_rows(log_f, carry[...])
    if tail_at is not None:
        c_ref[...] = c
    for head in range(N_HEADS):
        lo = head * 2 * HEAD_DIM + HEAD_DIM
        k_cols, q_cols = _bias_columns(c, head)
        kf_ref[:, lo:lo + HEAD_DIM] = k_cols.astype(BF16)
        qft_ref[lo:lo + HEAD_DIM, :] = q_cols.T.astype(BF16)

    vt_ref[...] = _dot(h, wqkv_ref[:, 2 * D_MODEL:3 * D_MODEL]).T.astype(BF16)
    q = _dot(h, wqkv_ref[:, 0:D_MODEL]) * Q_SCALE
    k = _dot(h, wqkv_ref[:, D_MODEL:2 * D_MODEL]).astype(BF16)
    for head in range(N_HEADS):
        src = slice(head * HEAD_DIM, (head + 1) * HEAD_DIM)
        dst = slice(head * 2 * HEAD_DIM, head * 2 * HEAD_DIM + HEAD_DIM)
        qft_ref[dst, :] = q[:, src].T.astype(BF16)
        kf_ref[:, dst] = k[:, src]
    yc_ref[...] = (_dot(h, wcv_ref[:, 0:D_MODEL]) * conv).astype(BF16)


def _mixer_in(x, g_mix, w_in, w_f, b_f, conv_w, halo, c0, *, tile, tail_at=None):
    b, n, _ = x.shape
    rows = pl.BlockSpec((None, tile, D_MODEL), lambda bi, i: (bi, i, 0))
    out_shape = [jax.ShapeDtypeStruct((b, 2 * D_MODEL, n), BF16), jax.ShapeDtypeStruct((b, n, 2 * D_MODEL), BF16),
                 jax.ShapeDtypeStruct((b, D_MODEL, n), BF16), jax.ShapeDtypeStruct((b, n, D_MODEL), BF16)]
    out_specs = [pl.BlockSpec((None, 2 * D_MODEL, tile), lambda bi, i: (bi, 0, i)),
                 pl.BlockSpec((None, tile, 2 * D_MODEL), lambda bi, i: (bi, i, 0)),
                 pl.BlockSpec((None, D_MODEL, tile), lambda bi, i: (bi, 0, i)), rows]
    if tail_at is not None:
        assert b == 1 and n == tile
        out_shape += [jax.ShapeDtypeStruct((b, n, LANES), F32), jax.ShapeDtypeStruct((HALO, D_MODEL), F32)]
        out_specs += [pl.BlockSpec((None, tile, LANES), lambda bi, i: (bi, i, 0)),
                      pl.BlockSpec((HALO, D_MODEL), lambda bi, i: (0, 0))]
    return pl.pallas_call(
        functools.partial(_mixer_in_kernel, tail_at=tail_at),
        grid=(b, n // tile),
        in_specs=[rows, _resident((1, D_MODEL)), _resident(*W_IN_BLOCKS["qkv"]), _resident(w_f.shape),
                  _resident(b_f.shape), _resident(*W_IN_BLOCKS["conv"]), _resident(conv_w.shape),
                  _resident(halo.shape), _resident(c0.shape)],
        out_specs=out_specs,
        out_shape=out_shape,
        scratch_shapes=[pltpu.VMEM((tile + HALO, D_MODEL), F32), pltpu.VMEM((1, LANES), F32)],
        compiler_params=pltpu.CompilerParams(dimension_semantics=("arbitrary", "arbitrary"),
                                             vmem_limit_bytes=VMEM_LIMIT),
        name="mixer_in" if tail_at is None else "mixer_in_meta",
    )(x, g_mix, w_in, w_f, b_f, w_in, conv_w, halo, c0)


def _attention_kernel(qft_ref, *rest, has_prefix, multi):
    rest = list(rest)
    qftn_ref = rest.pop(0) if multi else None
    kf_ref, vt_ref = rest.pop(0), rest.pop(0)
    kpf_ref, vtp_ref = (rest.pop(0), rest.pop(0)) if has_prefix else (None, None)
    o_ref, diag_scr, ab_scr, abmax_scr, m_scr, acc_scr = rest
    tq = qft_ref.shape[1]
    n_q = kf_ref.shape[0] // tq
    group = range(vt_ref.shape[0] // HEAD_DIM)
    i = pl.program_id(2)

    def cols(g):
        return slice(g * HEAD_DIM, (g + 1) * HEAD_DIM)

    def wide(g):
        return slice(g * 2 * HEAD_DIM, (g + 1) * 2 * HEAD_DIM)

    def rows_at(block):
        if isinstance(block, int):
            return slice(block * tq, (block + 1) * tq)
        return pl.ds(pl.multiple_of(block * tq, tq), tq)

    def ones_rows(keys):
        first_row = lax.broadcasted_iota(jnp.int32, (BF16_ROWS, keys), 0) == 0
        return jnp.where(first_row, 1.0, 0.0).astype(BF16)

    key = lax.broadcasted_iota(jnp.int32, (tq, tq), 0)
    qry = lax.broadcasted_iota(jnp.int32, (tq, tq), 1)
    ones_tq = ones_rows(tq)

    def q_cur(g):
        return qft_ref[wide(g), :]

    def produce(dst, block, q, max_dst=None):
        for g in group:
            s = _dot(kf_ref[rows_at(block), wide(g)], q(g))
            dst[g] = s
            if max_dst is not None:
                max_dst[g] = jnp.max(s, axis=0, keepdims=True)

    def update(g, s, vt_ones, s_max):
        m_old = m_scr[g]
        m_new = jnp.maximum(m_old, s_max)
        alpha = jnp.exp2(m_old - m_new)
        p = jnp.exp2(s - m_new).astype(BF16)
        acc_scr[g] = alpha * acc_scr[g] + _dot(vt_ones, p)
        m_scr[g] = m_new

    def prefix_scores():
        return [_dot(kpf_ref[:, wide(g)], q_cur(g)) for g in group] if has_prefix else None

    def consume(src, block, max_src):
        for g in group:
            vt_ones = jnp.concatenate([vt_ref[cols(g), rows_at(block)], ones_tq], axis=0)
            update(g, src[g], vt_ones, max_src[g])

    def start(s_prefix, block, diag):
        for g in group:
            s = jnp.where(key <= qry, diag[g], MASKED)
            vt = vt_ref[cols(g), rows_at(block)]
            if has_prefix:
                s = jnp.concatenate([s, s_prefix[g][0:N_META]], axis=0)
                vt = jnp.concatenate([vt, vtp_ref[cols(g), :]], axis=1)
            s_max = jnp.max(s, axis=0, keepdims=True)
            p = jnp.exp2(s - s_max).astype(BF16)
            if has_prefix:
                p = jnp.concatenate([p, jnp.zeros((vt.shape[1] - p.shape[0], tq), BF16)], axis=0)
            acc_scr[g] = _dot(jnp.concatenate([vt, ones_rows(vt.shape[1])], axis=0), p)
            m_scr[g] = s_max

    @pl.when(i == 0)
    def _():
        produce(diag_scr.at[0], 0, q_cur)

    if not multi:
        start(prefix_scores(), 0, diag_scr.at[0])
    else:
        def q_nxt(g):
            return qftn_ref[wide(g), :]

        def step(j, src, dst):
            produce(ab_scr.at[dst], j + 1, q_cur, abmax_scr.at[dst])
            consume(ab_scr.at[src], j, abmax_scr.at[src])

        def unrolled(blk):
            s_prefix = prefix_scores()
            if blk == 0:
                produce(diag_scr.at[1], 1, q_nxt)
                start(s_prefix, 0, diag_scr.at[0])
                return
            produce(ab_scr.at[0], 0, q_cur, abmax_scr.at[0])
            start(s_prefix, blk, diag_scr.at[blk & 1])
            for j in range(blk - 1):
                step(j, j & 1, (j + 1) & 1)
            produce(diag_scr.at[(blk + 1) & 1], blk + 1, q_nxt)
            consume(ab_scr.at[(blk - 1) & 1], blk - 1, abmax_scr.at[(blk - 1) & 1])

        def looped():
            s_prefix = prefix_scores()
            produce(ab_scr.at[(i - 1) & 1], 0, q_cur, abmax_scr.at[(i - 1) & 1])
            start(s_prefix, i, diag_scr.at[i & 1])

            @pl.when((i & 1) == 0)
            def _():
                step(0, 1, 0)

            def pair(jj, carry):
                j = 1 - (i & 1) + 2 * jj
                step(j, 0, 1)
                step(j + 1, 1, 0)
                return carry

            lax.fori_loop(0, lax.shift_right_logical(i - 1, 1), pair, 0)
            produce(diag_scr.at[(i + 1) & 1], jnp.minimum(i + 1, n_q - 1), q_nxt)
            consume(ab_scr.at[0], i - 1, abmax_scr.at[0])

        assert 2 <= UNROLLED_BLOCKS < n_q
        lax.switch(jnp.minimum(i, UNROLLED_BLOCKS),
                   [functools.partial(unrolled, blk) for blk in range(UNROLLED_BLOCKS)] + [looped])

    for g in group:
        acc = acc_scr[g]
        inv_l = 1.0 / acc[HEAD_DIM:HEAD_DIM + 1]
        o_ref[:, cols(g)] = (acc[0:HEAD_DIM] * inv_l).T.astype(o_ref.dtype)


def _attention(qft, kf, vt, prefix, *, tq, heads):
    b, n, _ = kf.shape
    n_q = n // tq
    width = heads * HEAD_DIM
    in_specs, args = [pl.BlockSpec((None, 2 * width, tq), lambda bi, h, i: (bi, h, i))], [qft]
    if n_q > 1:
        in_specs.append(pl.BlockSpec((None, 2 * width, tq), lambda bi, h, i: (bi, h, jnp.minimum(i + 1, n_q - 1))))
        args.append(qft)
    in_specs += [pl.BlockSpec((None, n, 2 * width), lambda bi, h, i: (bi, 0, h)),
                 pl.BlockSpec((None, width, n), lambda bi, h, i: (bi, h, 0))]
    args += [kf, vt]
    if prefix is not None:
        kpf, vtp = prefix
        n_p = kpf.shape[0]
        in_specs += [pl.BlockSpec((n_p, 2 * width), lambda bi, h, i: (0, h)),
                     pl.BlockSpec((width, n_p), lambda bi, h, i: (h, 0))]
        args += [kpf, vtp]
    scratch = [pltpu.VMEM((2, heads, tq, tq), F32), pltpu.VMEM((2, heads, tq, tq), F32),
               pltpu.VMEM((2, heads, 1, tq), F32), pltpu.VMEM((heads, 1, tq), F32),
               pltpu.VMEM((heads, HEAD_DIM + BF16_ROWS, tq), F32)]
    return pl.pallas_call(
        functools.partial(_attention_kernel, has_prefix=prefix is not None, multi=n_q > 1),
        grid=(b, N_HEADS // heads, n_q),
        in_specs=in_specs,
        out_specs=pl.BlockSpec((None, tq, width), lambda bi, h, i: (bi, i, h)),
        out_shape=jax.ShapeDtypeStruct((b, n, D_MODEL), BF16),
        scratch_shapes=scratch,
        compiler_params=pltpu.CompilerParams(dimension_semantics=("arbitrary",) * 3,
                                             vmem_limit_bytes=VMEM_LIMIT),
        name="attention" if prefix is not None else "attention_meta",
    )(*args)


def _mixer_out_kernel(x_ref, att_ref, yc_ref, gmix_ref, gffn_ref, gfin_ref, wg_ref, woc_ref, woa_ref,
                      wo_ref, wa_ref, wv_ref, wd_ref, fcw_ref, halo_ref, out_ref, a_scr, *, tail_at):
    t = x_ref.shape[0]

    @pl.when(pl.program_id(1) == 0)
    def _():
        a_scr[0:HALO, :] = halo_ref[...]

    x = x_ref[...]
    h = _rms_norm(x, gmix_ref[...]).astype(BF16)
    g_att = _sigmoid(_dot(h, wg_ref[:, 0:D_MODEL]))
    g_conv = _sigmoid(_dot(h, wg_ref[:, D_MODEL:2 * D_MODEL]))
    mix = g_att * _dot(att_ref[...], woa_ref[...]) + g_conv * _dot(yc_ref[...], woc_ref[...])
    z1 = x + _dot(mix.astype(BF16), wo_ref[...])
    h2 = _rms_norm(z1, gffn_ref[...]).astype(BF16)

    if tail_at is not None:
        a_scr[HALO:HALO + t, :] = _dot(h2, wa_ref[...])
        out_ref[...] = a_scr[tail_at:tail_at + HALO, :]
        return

    a_scr[HALO:HALO + t, :] = _dot(h2, wa_ref[...])
    conv = _causal_conv(a_scr, t, fcw_ref)
    act = conv * _sigmoid(conv) * _dot(h2, wv_ref[...])
    a_scr[0:HALO, :] = a_scr[t:t + HALO, :]
    out_ref[...] = _rms_norm(z1 + _dot(act.astype(BF16), wd_ref[...]), gfin_ref[...])


def _mixer_out(x, att, yc, g_mix, g_ffn, g_final, w_in, w_sq, w_ffn, w_d, ffn_conv_w, halo, *, tile, tail_at=None):
    b, n, _ = x.shape
    square = [_resident((None, D_MODEL, D_MODEL), (k, 0, 0)) for k in range(3)]
    halves = [_resident((D_MODEL, D_FF), (0, k)) for k in range(2)]
    rows = pl.BlockSpec((None, tile, D_MODEL), lambda bi, i: (bi, i, 0))
    if tail_at is not None:
        assert b == 1 and n == tile
        out_shape = jax.ShapeDtypeStruct((HALO, D_FF), F32)
        out_spec = pl.BlockSpec((HALO, D_FF), lambda bi, i: (0, 0))
    else:
        out_shape = jax.ShapeDtypeStruct(x.shape, F32)
        out_spec = rows
    consts = (g_mix, g_ffn, g_final, w_in, w_sq, w_sq, w_sq, w_ffn, w_ffn, w_d, ffn_conv_w, halo)
    const_specs = ([_resident(g_mix.shape)] * 3 + [_resident(*W_IN_BLOCKS["gates"])] + square + halves
                   + [_resident(w_d.shape), _resident(ffn_conv_w.shape), _resident(halo.shape)])
    return pl.pallas_call(
        functools.partial(_mixer_out_kernel, tail_at=tail_at),
        grid=(b, n // tile),
        in_specs=[rows, rows, rows] + const_specs,
        out_specs=out_spec,
        out_shape=out_shape,
        scratch_shapes=[pltpu.VMEM((tile + HALO, D_FF), F32)],
        compiler_params=pltpu.CompilerParams(dimension_semantics=("arbitrary", "arbitrary"),
                                             vmem_limit_bytes=VMEM_LIMIT),
        name="mixer_out" if tail_at is None else "mixer_out_meta",
    )(x, att, yc, *consts)


def kernel(x, meta_tokens, g_mix, w_in, b_f, conv_w, w_o_attn, w_o_conv, w_o, g_ffn, w_ffn_in, ffn_conv_w,
           w_ffn_out, g_final):
    assert w_in.shape[0] == 1, "one layer"
    d = D_MODEL
    w_in_packed, w_f = _repack_w_in(w_in[0].T)
    b_f_row = jnp.pad(b_f[0], (0, LANES - N_HEADS))[None, :]
    w_sq = jnp.stack([w_o_conv[0], w_o_attn[0], w_o[0]]).astype(BF16)
    w_ffn = w_ffn_in[0].astype(BF16)
    w_d = w_ffn_out[0].astype(BF16)
    g_mix_row, g_ffn_row, g_final_row = g_mix[0][None, :], g_ffn[0][None, :], g_final[None, :]
    in_consts = (g_mix_row, w_in_packed, w_f, b_f_row, conv_w[0])
    out_consts = (g_mix_row, g_ffn_row, g_final_row, w_in_packed, w_sq, w_ffn, w_d, ffn_conv_w[0])

    meta = jnp.pad(meta_tokens.astype(F32), ((0, META_TILE - N_META), (0, 0)))[None]
    qftm, kfm, vtm, ycm, cm, cu_tail = _mixer_in(
        meta, *in_consts, jnp.zeros((HALO, d), F32), jnp.zeros((1, LANES), F32), tile=META_TILE, tail_at=N_META)
    att_m = _attention(qftm, kfm, vtm, None, tq=META_TILE, heads=HEADS_PER_STEP)
    a_tail = _mixer_out(meta, att_m, ycm, *out_consts, jnp.zeros((HALO, D_FF), F32),
                        tile=META_TILE, tail_at=N_META)

    qft, kf, vt, yc = _mixer_in(x, *in_consts, cu_tail, cm[0, N_META - 1:N_META, :], tile=ROW_TILE_IN)
    att = _attention(qft, kf, vt, (kfm[0], vtm[0]), tq=Q_TILE, heads=HEADS_PER_STEP)
    return _mixer_out(x, att, yc, *out_consts, a_tail, tile=ROW_TILE_OUT)
```

```python
import functools

import jax
import jax.numpy as jnp
from jax import lax
from jax.experimental import pallas as pl
from jax.experimental.pallas import tpu as pltpu

D_MODEL = 1024
N_HEADS = 8
HEAD_DIM = 128
N_META = 16
D_FF = 2816
RMS_EPS = 1e-6

LANES = 128
SUBLANES = 8
BF16_ROWS = 2 * SUBLANES
HALO = SUBLANES
VMEM_LIMIT = 56 * 1024 * 1024

LOG2E = 1.4426950408889634
Q_SCALE = HEAD_DIM ** -0.5 * LOG2E
MASKED = -1e30

META_TILE = LANES
ROW_TILE_IN = 512
ROW_TILE_OUT = 512
Q_TILE = 512
HEADS_PER_STEP = 4
UNROLLED_BLOCKS = 6
W_IN_BLOCKS = {
    "qkv": ((D_MODEL, 3 * D_MODEL), (0, 0)),
    "conv": ((D_MODEL, 3 * D_MODEL), (0, 1)),
    "gates": ((D_MODEL, 2 * D_MODEL), (0, 3)),
}

BF16 = jnp.bfloat16
F32 = jnp.float32


def _dot(a, b):
    return jnp.dot(a, b, preferred_element_type=F32)


def _rms_norm(x, g):
    return x * lax.rsqrt(jnp.mean(x * x, axis=-1, keepdims=True) + RMS_EPS) * g


def _sigmoid(x):
    return 1.0 / (1.0 + jnp.exp(-x))


def _split3(x):
    hi = x.astype(BF16).astype(F32)
    mid = (x - hi).astype(BF16).astype(F32)
    lo = (x - hi - mid).astype(BF16).astype(F32)
    return hi, mid, lo


def _cumsum_rows(x, start):
    t = x.shape[0]
    row = lax.broadcasted_iota(jnp.int32, (LANES, LANES), 0)
    col = lax.broadcasted_iota(jnp.int32, (LANES, LANES), 1)
    tri = jnp.where(col <= row, 1.0, 0.0).astype(BF16)
    blocks = []
    for r in range(0, t, LANES):
        hi, mid, lo = _split3(x[r:r + LANES])
        local = _dot(tri, hi.astype(BF16)) + _dot(tri, mid.astype(BF16)) + _dot(tri, lo.astype(BF16))
        blocks.append(local + start)
        start = blocks[-1][LANES - 1:LANES]
    return jnp.concatenate(blocks, axis=0), start


def _causal_conv(scr, t, w_ref):
    out = scr[HALO - 2:HALO - 2 + t, :] * w_ref[0:1, :]
    out = out + scr[HALO - 1:HALO - 1 + t, :] * w_ref[1:2, :]
    return out + scr[HALO:HALO + t, :] * w_ref[2:3, :]


def _resident(block_shape, block_index=None):
    index = tuple(block_index) if block_index is not None else (0,) * len(block_shape)
    return pl.BlockSpec(block_shape, lambda *_: index, pipeline_mode=pl.Buffered(1))


def _repack_w_in_kernel(w_ref, f_ref, o_ref, of_ref):
    o_ref[...] = w_ref[...].T.astype(BF16)
    f_rows = jnp.concatenate([f_ref[...], jnp.zeros((LANES - N_HEADS, D_MODEL), F32)], axis=0)
    of_ref[...] = f_rows.T.astype(BF16)


def _repack_w_in(w_t):
    d = D_MODEL
    assert N_HEADS % SUBLANES == 0
    return pl.pallas_call(
        _repack_w_in_kernel,
        grid=(8,),
        in_specs=[pl.BlockSpec((pl.Element(d), pl.Element(d)),
                               lambda j: ((j * (d // N_HEADS) + jnp.where(j >= 3, 1, 0)) * N_HEADS, 0)),
                  pl.BlockSpec((N_HEADS, d), lambda j: (3 * d // N_HEADS, 0))],
        out_specs=[pl.BlockSpec((d, d), lambda j: (0, j)), pl.BlockSpec((d, LANES), lambda j: (0, 0))],
        out_shape=[jax.ShapeDtypeStruct((d, 8 * d), BF16), jax.ShapeDtypeStruct((d, LANES), BF16)],
        compiler_params=pltpu.CompilerParams(dimension_semantics=("arbitrary",), vmem_limit_bytes=VMEM_LIMIT),
        name="repack_w_in",
    )(w_t, w_t)


def _bias_columns(c, head):
    lane = lax.broadcasted_iota(jnp.int32, c.shape, 1)
    pieces = _split3(c[:, head:head + 1])
    k_side = jnp.where((lane >= 3) & (lane < 6), 1.0, 0.0)
    q_side = jnp.where(lane < 3, 1.0, 0.0)
    for n, piece in enumerate(pieces):
        k_side = jnp.where(lane == n, -piece, k_side)
        q_side = jnp.where(lane == 3 + n, piece, q_side)
    return k_side, q_side


def _mixer_in_kernel(x_ref, g_ref, wqkv_ref, wf_ref, bf_ref, wcv_ref, cw_ref, halo_ref, c0_ref,
                     qft_ref, kf_ref, vt_ref, yc_ref, *rest, tail_at):
    if tail_at is not None:
        c_ref, tail_ref, cu_scr, carry = rest
    else:
        cu_scr, carry = rest
    t = x_ref.shape[0]

    @pl.when(pl.program_id(1) == 0)
    def _():
        cu_scr[0:HALO, :] = halo_ref[...]
        carry[...] = c0_ref[...]

    h = _rms_norm(x_ref[...], g_ref[...]).astype(BF16)

    gc = _dot(h, wcv_ref[:, D_MODEL:2 * D_MODEL])
    u = _dot(h, wcv_ref[:, 2 * D_MODEL:3 * D_MODEL])
    cu_scr[HALO:HALO + t, :] = gc * u
    conv = _causal_conv(cu_scr, t, cw_ref)
    if tail_at is not None:
        tail_ref[...] = cu_scr[tail_at:tail_at + HALO, :]
    cu_scr[0:HALO, :] = cu_scr[t:t + HALO, :]

    f = _dot(h, wf_ref[...]) + bf_ref[...]
    log_f = (jnp.minimum(f, 0.0) - jnp.log1p(jnp.exp(-jnp.abs(f)))) * LOG2E
    c, carry[...] = _cumsum_rows(log_f, carry[...])
    if tail_at is not None:
        c_ref[...] = c
    for head in range(N_HEADS):
        lo = head * 2 * HEAD_DIM + HEAD_DIM
        k_cols, q_cols = _bias_columns(c, head)
        kf_ref[:, lo:lo + HEAD_DIM] = k_cols.astype(BF16)
        qft_ref[lo:lo + HEAD_DIM, :] = q_cols.T.astype(BF16)

    vt_ref[...] = _dot(h, wqkv_ref[:, 2 * D_MODEL:3 * D_MODEL]).T.astype(BF16)
    q = _dot(h, wqkv_ref[:, 0:D_MODEL]) * Q_SCALE
    k = _dot(h, wqkv_ref[:, D_MODEL:2 * D_MODEL]).astype(BF16)
    for head in range(N_HEADS):
        src = slice(head * HEAD_DIM, (head + 1) * HEAD_DIM)
        dst = slice(head * 2 * HEAD_DIM, head * 2 * HEAD_DIM + HEAD_DIM)
        qft_ref[dst, :] = q[:, src].T.astype(BF16)
        kf_ref[:, dst] = k[:, src]
    yc_ref[...] = (_dot(h, wcv_ref[:, 0:D_MODEL]) * conv).astype(BF16)


def _mixer_in(x, g_mix, w_in, w_f, b_f, conv_w, halo, c0, *, tile, tail_at=None):
    b, n, _ = x.shape
    rows = pl.BlockSpec((None, tile, D_MODEL), lambda bi, i: (bi, i, 0))
    out_shape = [jax.ShapeDtypeStruct((b, 2 * D_MODEL, n), BF16), jax.ShapeDtypeStruct((b, n, 2 * D_MODEL), BF16),
                 jax.ShapeDtypeStruct((b, D_MODEL, n), BF16), jax.ShapeDtypeStruct((b, n, D_MODEL), BF16)]
    out_specs = [pl.BlockSpec((None, 2 * D_MODEL, tile), lambda bi, i: (bi, 0, i)),
                 pl.BlockSpec((None, tile, 2 * D_MODEL), lambda bi, i: (bi, i, 0)),
                 pl.BlockSpec((None, D_MODEL, tile), lambda bi, i: (bi, 0, i)), rows]
    if tail_at is not None:
        assert b == 1 and n == tile
        out_shape += [jax.ShapeDtypeStruct((b, n, LANES), F32), jax.ShapeDtypeStruct((HALO, D_MODEL), F32)]
        out_specs += [pl.BlockSpec((None, tile, LANES), lambda bi, i: (bi, i, 0)),
                      pl.BlockSpec((HALO, D_MODEL), lambda bi, i: (0, 0))]
    return pl.pallas_call(
        functools.partial(_mixer_in_kernel, tail_at=tail_at),
        grid=(b, n // tile),
        in_specs=[rows, _resident((1, D_MODEL)), _resident(*W_IN_BLOCKS["qkv"]), _resident(w_f.shape),
                  _resident(b_f.shape), _resident(*W_IN_BLOCKS["conv"]), _resident(conv_w.shape),
                  _resident(halo.shape), _resident(c0.shape)],
        out_specs=out_specs,
        out_shape=out_shape,
        scratch_shapes=[pltpu.VMEM((tile + HALO, D_MODEL), F32), pltpu.VMEM((1, LANES), F32)],
        compiler_params=pltpu.CompilerParams(dimension_semantics=("arbitrary", "arbitrary"),
                                             vmem_limit_bytes=VMEM_LIMIT),
        name="mixer_in" if tail_at is None else "mixer_in_meta",
    )(x, g_mix, w_in, w_f, b_f, w_in, conv_w, halo, c0)


def _attention_kernel(qft_ref, *rest, has_prefix, multi):
    rest = list(rest)
    qftn_ref = rest.pop(0) if multi else None
    kf_ref, vt_ref = rest.pop(0), rest.pop(0)
    kpf_ref, vtp_ref = (rest.pop(0), rest.pop(0)) if has_prefix else (None, None)
    o_ref, diag_scr, ab_scr, abmax_scr, m_scr, acc_scr = rest
    tq = qft_ref.shape[1]
    n_q = kf_ref.shape[0] // tq
    group = range(vt_ref.shape[0] // HEAD_DIM)
    i = pl.program_id(2)

    def cols(g):
        return slice(g * HEAD_DIM, (g + 1) * HEAD_DIM)

    def wide(g):
        return slice(g * 2 * HEAD_DIM, (g + 1) * 2 * HEAD_DIM)

    def rows_at(block):
        if isinstance(block, int):
            return slice(block * tq, (block + 1) * tq)
        return pl.ds(pl.multiple_of(block * tq, tq), tq)

    def ones_rows(keys):
        first_row = lax.broadcasted_iota(jnp.int32, (BF16_ROWS, keys), 0) == 0
        return jnp.where(first_row, 1.0, 0.0).astype(BF16)

    key = lax.broadcasted_iota(jnp.int32, (tq, tq), 0)
    qry = lax.broadcasted_iota(jnp.int32, (tq, tq), 1)
    ones_tq = ones_rows(tq)

    def q_cur(g):
        return qft_ref[wide(g), :]

    def produce(dst, block, q, max_dst=None):
        for g in group:
            s = _dot(kf_ref[rows_at(block), wide(g)], q(g))
            dst[g] = s
            if max_dst is not None:
                max_dst[g] = jnp.max(s, axis=0, keepdims=True)

    def update(g, s, vt_ones, s_max):
        m_old = m_scr[g]
        m_new = jnp.maximum(m_old, s_max)
        alpha = jnp.exp2(m_old - m_new)
        p = jnp.exp2(s - m_new).astype(BF16)
        acc_scr[g] = alpha * acc_scr[g] + _dot(vt_ones, p)
        m_scr[g] = m_new

    def prefix_scores():
        return [_dot(kpf_ref[:, wide(g)], q_cur(g)) for g in group] if has_prefix else None

    def consume(src, block, max_src):
        for g in group:
            vt_ones = jnp.concatenate([vt_ref[cols(g), rows_at(block)], ones_tq], axis=0)
            update(g, src[g], vt_ones, max_src[g])

    def start(s_prefix, block, diag):
        for g in group:
            s = jnp.where(key <= qry, diag[g], MASKED)
            vt = vt_ref[cols(g), rows_at(block)]
            if has_prefix:
                s = jnp.concatenate([s, s_prefix[g][0:N_META]], axis=0)
                vt = jnp.concatenate([vt, vtp_ref[cols(g), :]], axis=1)
            s_max = jnp.max(s, axis=0, keepdims=True)
            p = jnp.exp2(s - s_max).astype(BF16)
            if has_prefix:
                p = jnp.concatenate([p, jnp.zeros((vt.shape[1] - p.shape[0], tq), BF16)], axis=0)
            acc_scr[g] = _dot(jnp.concatenate([vt, ones_rows(vt.shape[1])], axis=0), p)
            m_scr[g] = s_max

    @pl.when(i == 0)
    def _():
        produce(diag_scr.at[0], 0, q_cur)

    if not multi:
        start(prefix_scores(), 0, diag_scr.at[0])
    else:
        def q_nxt(g):
            return qftn_ref[wide(g), :]

        def step(j, src, dst):
            produce(ab_scr.at[dst], j + 1, q_cur, abmax_scr.at[dst])
            consume(ab_scr.at[src], j, abmax_scr.at[src])

        def unrolled(blk):
            s_prefix = prefix_scores()
            if blk == 0:
                produce(diag_scr.at[1], 1, q_nxt)
                start(s_prefix, 0, diag_scr.at[0])
                return
            produce(ab_scr.at[0], 0, q_cur, abmax_scr.at[0])
            start(s_prefix, blk, diag_scr.at[blk & 1])
            for j in range(blk - 1):
                step(j, j & 1, (j + 1) & 1)
            produce(diag_scr.at[(blk + 1) & 1], blk + 1, q_nxt)
            consume(ab_scr.at[(blk - 1) & 1], blk - 1, abmax_scr.at[(blk - 1) & 1])

        def looped():
            s_prefix = prefix_scores()
            produce(ab_scr.at[(i - 1) & 1], 0, q_cur, abmax_scr.at[(i - 1) & 1])
            start(s_prefix, i, diag_scr.at[i & 1])

            @pl.when((i & 1) == 0)
            def _():
                step(0, 1, 0)

            def pair(jj, carry):
                j = 1 - (i & 1) + 2 * jj
                step(j, 0, 1)
                step(j + 1, 1, 0)
                return carry

            lax.fori_loop(0, lax.shift_right_logical(i - 1, 1), pair, 0)
            produce(diag_scr.at[(i + 1) & 1], jnp.minimum(i + 1, n_q - 1), q_nxt)
            consume(ab_scr.at[0], i - 1, abmax_scr.at[0])

        assert 2 <= UNROLLED_BLOCKS < n_q
        lax.switch(jnp.minimum(i, UNROLLED_BLOCKS),
                   [functools.partial(unrolled, blk) for blk in range(UNROLLED_BLOCKS)] + [looped])

    for g in group:
        acc = acc_scr[g]
        inv_l = 1.0 / acc[HEAD_DIM:HEAD_DIM + 1]
        o_ref[:, cols(g)] = (acc[0:HEAD_DIM] * inv_l).T.astype(o_ref.dtype)


def _attention(qft, kf, vt, prefix, *, tq, heads):
    b, n, _ = kf.shape
    n_q = n // tq
    width = heads * HEAD_DIM
    in_specs, args = [pl.BlockSpec((None, 2 * width, tq), lambda bi, h, i: (bi, h, i))], [qft]
    if n_q > 1:
        in_specs.append(pl.BlockSpec((None, 2 * width, tq), lambda bi, h, i: (bi, h, jnp.minimum(i + 1, n_q - 1))))
        args.append(qft)
    in_specs += [pl.BlockSpec((None, n, 2 * width), lambda bi, h, i: (bi, 0, h)),
                 pl.BlockSpec((None, width, n), lambda bi, h, i: (bi, h, 0))]
    args += [kf, vt]
    if prefix is not None:
        kpf, vtp = prefix
        n_p = kpf.shape[0]
        in_specs += [pl.BlockSpec((n_p, 2 * width), lambda bi, h, i: (0, h)),
                     pl.BlockSpec((width, n_p), lambda bi, h, i: (h, 0))]
        args += [kpf, vtp]
    scratch = [pltpu.VMEM((2, heads, tq, tq), F32), pltpu.VMEM((2, heads, tq, tq), F32),
               pltpu.VMEM((2, heads, 1, tq), F32), pltpu.VMEM((heads, 1, tq), F32),
               pltpu.VMEM((heads, HEAD_DIM + BF16_ROWS, tq), F32)]
    return pl.pallas_call(
        functools.partial(_attention_kernel, has_prefix=prefix is not None, multi=n_q > 1),
        grid=(b, N_HEADS // heads, n_q),
        in_specs=in_specs,
        out_specs=pl.BlockSpec((None, tq, width), lambda bi, h, i: (bi, i, h)),
        out_shape=jax.ShapeDtypeStruct((b, n, D_MODEL), BF16),
        scratch_shapes=scratch,
        compiler_params=pltpu.CompilerParams(dimension_semantics=("arbitrary",) * 3,
                                             vmem_limit_bytes=VMEM_LIMIT),
        name="attention" if prefix is not None else "attention_meta",
    )(*args)


def _mixer_out_kernel(x_ref, att_ref, yc_ref, gmix_ref, gffn_ref, gfin_ref, wg_ref, woc_ref, woa_ref,
                      wo_ref, wa_ref, wv_ref, wd_ref, fcw_ref, halo_ref, out_ref, a_scr, *, tail_at):
    t = x_ref.shape[0]

    @pl.when(pl.program_id(1) == 0)
    def _():
        a_scr[0:HALO, :] = halo_ref[...]

    x = x_ref[...]
    h = _rms_norm(x, gmix_ref[...]).astype(BF16)
    g_att = _sigmoid(_dot(h, wg_ref[:, 0:D_MODEL]))
    g_conv = _sigmoid(_dot(h, wg_ref[:, D_MODEL:2 * D_MODEL]))
    mix = g_att * _dot(att_ref[...], woa_ref[...]) + g_conv * _dot(yc_ref[...], woc_ref[...])
    z1 = x + _dot(mix.astype(BF16), wo_ref[...])
    h2 = _rms_norm(z1, gffn_ref[...]).astype(BF16)

    if tail_at is not None:
        a_scr[HALO:HALO + t, :] = _dot(h2, wa_ref[...])
        out_ref[...] = a_scr[tail_at:tail_at + HALO, :]
        return

    a_scr[HALO:HALO + t, :] = _dot(h2, wa_ref[...])
    conv = _causal_conv(a_scr, t, fcw_ref)
    act = conv * _sigmoid(conv) * _dot(h2, wv_ref[...])
    a_scr[0:HALO, :] = a_scr[t:t + HALO, :]
    out_ref[...] = _rms_norm(z1 + _dot(act.astype(BF16), wd_ref[...]), gfin_ref[...])


def _mixer_out(x, att, yc, g_mix, g_ffn, g_final, w_in, w_sq, w_ffn, w_d, ffn_conv_w, halo, *, tile, tail_at=None):
    b, n, _ = x.shape
    square = [_resident((None, D_MODEL, D_MODEL), (k, 0, 0)) for k in range(3)]
    halves = [_resident((D_MODEL, D_FF), (0, k)) for k in range(2)]
    rows = pl.BlockSpec((None, tile, D_MODEL), lambda bi, i: (bi, i, 0))
    if tail_at is not None:
        assert b == 1 and n == tile
        out_shape = jax.ShapeDtypeStruct((HALO, D_FF), F32)
        out_spec = pl.BlockSpec((HALO, D_FF), lambda bi, i: (0, 0))
    else:
        out_shape = jax.ShapeDtypeStruct(x.shape, F32)
        out_spec = rows
    consts = (g_mix, g_ffn, g_final, w_in, w_sq, w_sq, w_sq, w_ffn, w_ffn, w_d, ffn_conv_w, halo)
    const_specs = ([_resident(g_mix.shape)] * 3 + [_resident(*W_IN_BLOCKS["gates"])] + square + halves
                   + [_resident(w_d.shape), _resident(ffn_conv_w.shape), _resident(halo.shape)])
    return pl.pallas_call(
        functools.partial(_mixer_out_kernel, tail_at=tail_at),
        grid=(b, n // tile),
        in_specs=[rows, rows, rows] + const_specs,
        out_specs=out_spec,
        out_shape=out_shape,
        scratch_shapes=[pltpu.VMEM((tile + HALO, D_FF), F32)],
        compiler_params=pltpu.CompilerParams(dimension_semantics=("arbitrary", "arbitrary"),
                                             vmem_limit_bytes=VMEM_LIMIT),
        name="mixer_out" if tail_at is None else "mixer_out_meta",
    )(x, att, yc, *consts)


def kernel(x, meta_tokens, g_mix, w_in, b_f, conv_w, w_o_attn, w_o_conv, w_o, g_ffn, w_ffn_in, ffn_conv_w,
           w_ffn_out, g_final):
    assert w_in.shape[0] == 1, "one layer"
    d = D_MODEL
    w_in_packed, w_f = _repack_w_in(w_in[0].T)
    b_f_row = jnp.pad(b_f[0], (0, LANES - N_HEADS))[None, :]
    w_sq = jnp.stack([w_o_conv[0], w_o_attn[0], w_o[0]]).astype(BF16)
    w_ffn = w_ffn_in[0].astype(BF16)
    w_d = w_ffn_out[0].astype(BF16)
    g_mix_row, g_ffn_row, g_final_row = g_mix[0][None, :], g_ffn[0][None, :], g_final[None, :]
    in_consts = (g_mix_row, w_in_packed, w_f, b_f_row, conv_w[0])
    out_consts = (g_mix_row, g_ffn_row, g_final_row, w_in_packed, w_sq, w_ffn, w_d, ffn_conv_w[0])

    meta = jnp.pad(meta_tokens.astype(F32), ((0, META_TILE - N_META), (0, 0)))[None]
    qftm, kfm, vtm, ycm, cm, cu_tail = _mixer_in(
        meta, *in_consts, jnp.zeros((HALO, d), F32), jnp.zeros((1, LANES), F32), tile=META_TILE, tail_at=N_META)
    att_m = _attention(qftm, kfm, vtm, None, tq=META_TILE, heads=HEADS_PER_STEP)
    a_tail = _mixer_out(meta, att_m, ycm, *out_consts, jnp.zeros((HALO, D_FF), F32),
                        tile=META_TILE, tail_at=N_META)

    qft, kf, vt, yc = _mixer_in(x, *in_consts, cu_tail, cm[0, N_META - 1:N_META, :], tile=ROW_TILE_IN)
    att = _attention(qft, kf, vt, (kfm[0], vtm[0]), tq=Q_TILE, heads=HEADS_PER_STEP)
    return _mixer_out(x, att, yc, *out_consts, a_tail, tile=ROW_TILE_OUT)
```

```python
import functools

import jax
import jax.numpy as jnp
from jax import lax
from jax.experimental import pallas as pl
from jax.experimental.pallas import tpu as pltpu

D_MODEL = 1024
N_HEADS = 8
HEAD_DIM = 128
N_META = 16
D_FF = 2816
RMS_EPS = 1e-6

LANES = 128
SUBLANES = 8
BF16_ROWS = 2 * SUBLANES
HALO = SUBLANES
VMEM_LIMIT = 56 * 1024 * 1024

LOG2E = 1.4426950408889634
Q_SCALE = HEAD_DIM ** -0.5 * LOG2E
MASKED = -1e30

META_TILE = LANES
ROW_TILE_IN = 512
ROW_TILE_OUT = 512
Q_TILE = 512
HEADS_PER_STEP = 4
UNROLLED_BLOCKS = 7
W_IN_BLOCKS = {
    "qkv": ((D_MODEL, 3 * D_MODEL), (0, 0)),
    "conv": ((D_MODEL, 3 * D_MODEL), (0, 1)),
    "gates": ((D_MODEL, 2 * D_MODEL), (0, 3)),
}

BF16 = jnp.bfloat16
F32 = jnp.float32


def _dot(a, b):
    return jnp.dot(a, b, preferred_element_type=F32)


def _rms_norm(x, g):
    return x * lax.rsqrt(jnp.mean(x * x, axis=-1, keepdims=True) + RMS_EPS) * g


def _sigmoid(x):
    return 1.0 / (1.0 + jnp.exp(-x))


def _split3(x):
    hi = x.astype(BF16).astype(F32)
    mid = (x - hi).astype(BF16).astype(F32)
    lo = (x - hi - mid).astype(BF16).astype(F32)
    return hi, mid, lo


def _cumsum_rows(x, start):
    t = x.shape[0]
    row = lax.broadcasted_iota(jnp.int32, (LANES, LANES), 0)
    col = lax.broadcasted_iota(jnp.int32, (LANES, LANES), 1)
    tri = jnp.where(col <= row, 1.0, 0.0).astype(BF16)
    blocks = []
    for r in range(0, t, LANES):
        hi, mid, lo = _split3(x[r:r + LANES])
        local = _dot(tri, hi.astype(BF16)) + _dot(tri, mid.astype(BF16)) + _dot(tri, lo.astype(BF16))
        blocks.append(local + start)
        start = blocks[-1][LANES - 1:LANES]
    return jnp.concatenate(blocks, axis=0), start


def _causal_conv(scr, t, w_ref):
    out = scr[HALO - 2:HALO - 2 + t, :] * w_ref[0:1, :]
    out = out + scr[HALO - 1:HALO - 1 + t, :] * w_ref[1:2, :]
    return out + scr[HALO:HALO + t, :] * w_ref[2:3, :]


def _resident(block_shape, block_index=None):
    index = tuple(block_index) if block_index is not None else (0,) * len(block_shape)
    return pl.BlockSpec(block_shape, lambda *_: index, pipeline_mode=pl.Buffered(1))


def _repack_w_in_kernel(w_ref, f_ref, o_ref, of_ref):
    o_ref[...] = w_ref[...].T.astype(BF16)
    f_rows = jnp.concatenate([f_ref[...], jnp.zeros((LANES - N_HEADS, D_MODEL), F32)], axis=0)
    of_ref[...] = f_rows.T.astype(BF16)


def _repack_w_in(w_t):
    d = D_MODEL
    assert N_HEADS % SUBLANES == 0
    return pl.pallas_call(
        _repack_w_in_kernel,
        grid=(8,),
        in_specs=[pl.BlockSpec((pl.Element(d), pl.Element(d)),
                               lambda j: ((j * (d // N_HEADS) + jnp.where(j >= 3, 1, 0)) * N_HEADS, 0)),
                  pl.BlockSpec((N_HEADS, d), lambda j: (3 * d // N_HEADS, 0))],
        out_specs=[pl.BlockSpec((d, d), lambda j: (0, j)), pl.BlockSpec((d, LANES), lambda j: (0, 0))],
        out_shape=[jax.ShapeDtypeStruct((d, 8 * d), BF16), jax.ShapeDtypeStruct((d, LANES), BF16)],
        compiler_params=pltpu.CompilerParams(dimension_semantics=("arbitrary",), vmem_limit_bytes=VMEM_LIMIT),
        name="repack_w_in",
    )(w_t, w_t)


def _bias_columns(c, head):
    lane = lax.broadcasted_iota(jnp.int32, c.shape, 1)
    pieces = _split3(c[:, head:head + 1])
    k_side = jnp.where((lane >= 3) & (lane < 6), 1.0, 0.0)
    q_side = jnp.where(lane < 3, 1.0, 0.0)
    for n, piece in enumerate(pieces):
        k_side = jnp.where(lane == n, -piece, k_side)
        q_side = jnp.where(lane == 3 + n, piece, q_side)
    return k_side, q_side


def _mixer_in_kernel(x_ref, g_ref, wqkv_ref, wf_ref, bf_ref, wcv_ref, cw_ref, halo_ref, c0_ref,
                     qft_ref, kf_ref, vt_ref, yc_ref, *rest, tail_at):
    if tail_at is not None:
        c_ref, tail_ref, cu_scr, carry = rest
    else:
        cu_scr, carry = rest
    t = x_ref.shape[0]

    @pl.when(pl.program_id(1) == 0)
    def _():
        cu_scr[0:HALO, :] = halo_ref[...]
        carry[...] = c0_ref[...]

    h = _rms_norm(x_ref[...], g_ref[...]).astype(BF16)

    gc = _dot(h, wcv_ref[:, D_MODEL:2 * D_MODEL])
    u = _dot(h, wcv_ref[:, 2 * D_MODEL:3 * D_MODEL])
    cu_scr[HALO:HALO + t, :] = gc * u
    conv = _causal_conv(cu_scr, t, cw_ref)
    if tail_at is not None:
        tail_ref[...] = cu_scr[tail_at:tail_at + HALO, :]
    cu_scr[0:HALO, :] = cu_scr[t:t + HALO, :]

    f = _dot(h, wf_ref[...]) + bf_ref[...]
    log_f = (jnp.minimum(f, 0.0) - jnp.log1p(jnp.exp(-jnp.abs(f)))) * LOG2E
    c, carry[...] = _cumsum_rows(log_f, carry[...])
    if tail_at is not None:
        c_ref[...] = c
    for head in range(N_HEADS):
        lo = head * 2 * HEAD_DIM + HEAD_DIM
        k_cols, q_cols = _bias_columns(c, head)
        kf_ref[:, lo:lo + HEAD_DIM] = k_cols.astype(BF16)
        qft_ref[lo:lo + HEAD_DIM, :] = q_cols.T.astype(BF16)

    vt_ref[...] = _dot(h, wqkv_ref[:, 2 * D_MODEL:3 * D_MODEL]).T.astype(BF16)
    q = _dot(h, wqkv_ref[:, 0:D_MODEL]) * Q_SCALE
    k = _dot(h, wqkv_ref[:, D_MODEL:2 * D_MODEL]).astype(BF16)
    for head in range(N_HEADS):
        src = slice(head * HEAD_DIM, (head + 1) * HEAD_DIM)
        dst = slice(head * 2 * HEAD_DIM, head * 2 * HEAD_DIM + HEAD_DIM)
        qft_ref[dst, :] = q[:, src].T.astype(BF16)
        kf_ref[:, dst] = k[:, src]
    yc_ref[...] = (_dot(h, wcv_ref[:, 0:D_MODEL]) * conv).astype(BF16)


def _mixer_in(x, g_mix, w_in, w_f, b_f, conv_w, halo, c0, *, tile, tail_at=None):
    b, n, _ = x.shape
    rows = pl.BlockSpec((None, tile, D_MODEL), lambda bi, i: (bi, i, 0))
    out_shape = [jax.ShapeDtypeStruct((b, 2 * D_MODEL, n), BF16), jax.ShapeDtypeStruct((b, n, 2 * D_MODEL), BF16),
                 jax.ShapeDtypeStruct((b, D_MODEL, n), BF16), jax.ShapeDtypeStruct((b, n, D_MODEL), BF16)]
    out_specs = [pl.BlockSpec((None, 2 * D_MODEL, tile), lambda bi, i: (bi, 0, i)),
                 pl.BlockSpec((None, tile, 2 * D_MODEL), lambda bi, i: (bi, i, 0)),
                 pl.BlockSpec((None, D_MODEL, tile), lambda bi, i: (bi, 0, i)), rows]
    if tail_at is not None:
        assert b == 1 and n == tile
        out_shape += [jax.ShapeDtypeStruct((b, n, LANES), F32), jax.ShapeDtypeStruct((HALO, D_MODEL), F32)]
        out_specs += [pl.BlockSpec((None, tile, LANES), lambda bi, i: (bi, i, 0)),
                      pl.BlockSpec((HALO, D_MODEL), lambda bi, i: (0, 0))]
    return pl.pallas_call(
        functools.partial(_mixer_in_kernel, tail_at=tail_at),
        grid=(b, n // tile),
        in_specs=[rows, _resident((1, D_MODEL)), _resident(*W_IN_BLOCKS["qkv"]), _resident(w_f.shape),
                  _resident(b_f.shape), _resident(*W_IN_BLOCKS["conv"]), _resident(conv_w.shape),
                  _resident(halo.shape), _resident(c0.shape)],
        out_specs=out_specs,
        out_shape=out_shape,
        scratch_shapes=[pltpu.VMEM((tile + HALO, D_MODEL), F32), pltpu.VMEM((1, LANES), F32)],
        compiler_params=pltpu.CompilerParams(dimension_semantics=("arbitrary", "arbitrary"),
                                             vmem_limit_bytes=VMEM_LIMIT),
        name="mixer_in" if tail_at is None else "mixer_in_meta",
    )(x, g_mix, w_in, w_f, b_f, w_in, conv_w, halo, c0)


def _attention_kernel(qft_ref, *rest, has_prefix, multi):
    rest = list(rest)
    qftn_ref = rest.pop(0) if multi else None
    kf_ref, vt_ref = rest.pop(0), rest.pop(0)
    kpf_ref, vtp_ref = (rest.pop(0), rest.pop(0)) if has_prefix else (None, None)
    o_ref, diag_scr, ab_scr, abmax_scr, m_scr, acc_scr = rest
    tq = qft_ref.shape[1]
    n_q = kf_ref.shape[0] // tq
    group = range(vt_ref.shape[0] // HEAD_DIM)
    i = pl.program_id(2)

    def cols(g):
        return slice(g * HEAD_DIM, (g + 1) * HEAD_DIM)

    def wide(g):
        return slice(g * 2 * HEAD_DIM, (g + 1) * 2 * HEAD_DIM)

    def rows_at(block):
        if isinstance(block, int):
            return slice(block * tq, (block + 1) * tq)
        return pl.ds(pl.multiple_of(block * tq, tq), tq)

    def ones_rows(keys):
        first_row = lax.broadcasted_iota(jnp.int32, (BF16_ROWS, keys), 0) == 0
        return jnp.where(first_row, 1.0, 0.0).astype(BF16)

    key = lax.broadcasted_iota(jnp.int32, (tq, tq), 0)
    qry = lax.broadcasted_iota(jnp.int32, (tq, tq), 1)
    ones_tq = ones_rows(tq)

    def q_cur(g):
        return qft_ref[wide(g), :]

    def produce(dst, block, q, max_dst=None):
        for g in group:
            s = _dot(kf_ref[rows_at(block), wide(g)], q(g))
            dst[g] = s
            if max_dst is not None:
                max_dst[g] = jnp.max(s, axis=0, keepdims=True)

    def update(g, s, vt_ones, s_max):
        m_old = m_scr[g]
        m_new = jnp.maximum(m_old, s_max)
        alpha = jnp.exp2(m_old - m_new)
        p = jnp.exp2(s - m_new).astype(BF16)
        acc_scr[g] = alpha * acc_scr[g] + _dot(vt_ones, p)
        m_scr[g] = m_new

    def prefix_scores():
        return [_dot(kpf_ref[:, wide(g)], q_cur(g)) for g in group] if has_prefix else None

    def consume(src, block, max_src):
        for g in group:
            vt_ones = jnp.concatenate([vt_ref[cols(g), rows_at(block)], ones_tq], axis=0)
            update(g, src[g], vt_ones, max_src[g])

    def start(s_prefix, block, diag):
        for g in group:
            s = jnp.where(key <= qry, diag[g], MASKED)
            vt = vt_ref[cols(g), rows_at(block)]
            if has_prefix:
                s = jnp.concatenate([s, s_prefix[g][0:N_META]], axis=0)
                vt = jnp.concatenate([vt, vtp_ref[cols(g), :]], axis=1)
            s_max = jnp.max(s, axis=0, keepdims=True)
            p = jnp.exp2(s - s_max).astype(BF16)
            if has_prefix:
                p = jnp.concatenate([p, jnp.zeros((vt.shape[1] - p.shape[0], tq), BF16)], axis=0)
            acc_scr[g] = _dot(jnp.concatenate([vt, ones_rows(vt.shape[1])], axis=0), p)
            m_scr[g] = s_max

    @pl.when(i == 0)
    def _():
        produce(diag_scr.at[0], 0, q_cur)

    if not multi:
        start(prefix_scores(), 0, diag_scr.at[0])
    else:
        def q_nxt(g):
            return qftn_ref[wide(g), :]

        def step(j, src, dst):
            produce(ab_scr.at[dst], j + 1, q_cur, abmax_scr.at[dst])
            consume(ab_scr.at[src], j, abmax_scr.at[src])

        def unrolled(blk):
            s_prefix = prefix_scores()
            if blk == 0:
                produce(diag_scr.at[1], 1, q_nxt)
                start(s_prefix, 0, diag_scr.at[0])
                return
            produce(ab_scr.at[0], 0, q_cur, abmax_scr.at[0])
            start(s_prefix, blk, diag_scr.at[blk & 1])
            for j in range(blk - 1):
                step(j, j & 1, (j + 1) & 1)
            produce(diag_scr.at[(blk + 1) & 1], blk + 1, q_nxt)
            consume(ab_scr.at[(blk - 1) & 1], blk - 1, abmax_scr.at[(blk - 1) & 1])

        def looped():
            s_prefix = prefix_scores()
            produce(ab_scr.at[(i - 1) & 1], 0, q_cur, abmax_scr.at[(i - 1) & 1])
            start(s_prefix, i, diag_scr.at[i & 1])

            @pl.when((i & 1) == 0)
            def _():
                step(0, 1, 0)

            def pair(jj, carry):
                j = 1 - (i & 1) + 2 * jj
                step(j, 0, 1)
                step(j + 1, 1, 0)
                return carry

            lax.fori_loop(0, lax.shift_right_logical(i - 1, 1), pair, 0)
            produce(diag_scr.at[(i + 1) & 1], jnp.minimum(i + 1, n_q - 1), q_nxt)
            consume(ab_scr.at[0], i - 1, abmax_scr.at[0])

        assert 2 <= UNROLLED_BLOCKS < n_q
        lax.switch(jnp.minimum(i, UNROLLED_BLOCKS),
                   [functools.partial(unrolled, blk) for blk in range(UNROLLED_BLOCKS)] + [looped])

    for g in group:
        acc = acc_scr[g]
        inv_l = 1.0 / acc[HEAD_DIM:HEAD_DIM + 1]
        o_ref[:, cols(g)] = (acc[0:HEAD_DIM] * inv_l).T.astype(o_ref.dtype)


def _attention(qft, kf, vt, prefix, *, tq, heads):
    b, n, _ = kf.shape
    n_q = n // tq
    width = heads * HEAD_DIM
    in_specs, args = [pl.BlockSpec((None, 2 * width, tq), lambda bi, h, i: (bi, h, i))], [qft]
    if n_q > 1:
        in_specs.append(pl.BlockSpec((None, 2 * width, tq), lambda bi, h, i: (bi, h, jnp.minimum(i + 1, n_q - 1))))
        args.append(qft)
    in_specs += [pl.BlockSpec((None, n, 2 * width), lambda bi, h, i: (bi, 0, h)),
                 pl.BlockSpec((None, width, n), lambda bi, h, i: (bi, h, 0))]
    args += [kf, vt]
    if prefix is not None:
        kpf, vtp = prefix
        n_p = kpf.shape[0]
        in_specs += [pl.BlockSpec((n_p, 2 * width), lambda bi, h, i: (0, h)),
                     pl.BlockSpec((width, n_p), lambda bi, h, i: (h, 0))]
        args += [kpf, vtp]
    scratch = [pltpu.VMEM((2, heads, tq, tq), F32), pltpu.VMEM((2, heads, tq, tq), F32),
               pltpu.VMEM((2, heads, 1, tq), F32), pltpu.VMEM((heads, 1, tq), F32),
               pltpu.VMEM((heads, HEAD_DIM + BF16_ROWS, tq), F32)]
    return pl.pallas_call(
        functools.partial(_attention_kernel, has_prefix=prefix is not None, multi=n_q > 1),
        grid=(b, N_HEADS // heads, n_q),
        in_specs=in_specs,
        out_specs=pl.BlockSpec((None, tq, width), lambda bi, h, i: (bi, i, h)),
        out_shape=jax.ShapeDtypeStruct((b, n, D_MODEL), BF16),
        scratch_shapes=scratch,
        compiler_params=pltpu.CompilerParams(dimension_semantics=("arbitrary",) * 3,
                                             vmem_limit_bytes=VMEM_LIMIT),
        name="attention" if prefix is not None else "attention_meta",
    )(*args)


def _mixer_out_kernel(x_ref, att_ref, yc_ref, gmix_ref, gffn_ref, gfin_ref, wg_ref, woc_ref, woa_ref,
                      wo_ref, wa_ref, wv_ref, wd_ref, fcw_ref, halo_ref, out_ref, a_scr, *, tail_at):
    t = x_ref.shape[0]

    @pl.when(pl.program_id(1) == 0)
    def _():
        a_scr[0:HALO, :] = halo_ref[...]

    x = x_ref[...]
    h = _rms_norm(x, gmix_ref[...]).astype(BF16)
    g_att = _sigmoid(_dot(h, wg_ref[:, 0:D_MODEL]))
    g_conv = _sigmoid(_dot(h, wg_ref[:, D_MODEL:2 * D_MODEL]))
    mix = g_att * _dot(att_ref[...], woa_ref[...]) + g_conv * _dot(yc_ref[...], woc_ref[...])
    z1 = x + _dot(mix.astype(BF16), wo_ref[...])
    h2 = _rms_norm(z1, gffn_ref[...]).astype(BF16)

    if tail_at is not None:
        a_scr[HALO:HALO + t, :] = _dot(h2, wa_ref[...])
        out_ref[...] = a_scr[tail_at:tail_at + HALO, :]
        return

    a_scr[HALO:HALO + t, :] = _dot(h2, wa_ref[...])
    conv = _causal_conv(a_scr, t, fcw_ref)
    act = conv * _sigmoid(conv) * _dot(h2, wv_ref[...])
    a_scr[0:HALO, :] = a_scr[t:t + HALO, :]
    out_ref[...] = _rms_norm(z1 + _dot(act.astype(BF16), wd_ref[...]), gfin_ref[...])


def _mixer_out(x, att, yc, g_mix, g_ffn, g_final, w_in, w_sq, w_ffn, w_d, ffn_conv_w, halo, *, tile, tail_at=None):
    b, n, _ = x.shape
    square = [_resident((None, D_MODEL, D_MODEL), (k, 0, 0)) for k in range(3)]
    halves = [_resident((D_MODEL, D_FF), (0, k)) for k in range(2)]
    rows = pl.BlockSpec((None, tile, D_MODEL), lambda bi, i: (bi, i, 0))
    if tail_at is not None:
        assert b == 1 and n == tile
        out_shape = jax.ShapeDtypeStruct((HALO, D_FF), F32)
        out_spec = pl.BlockSpec((HALO, D_FF), lambda bi, i: (0, 0))
    else:
        out_shape = jax.ShapeDtypeStruct(x.shape, F32)
        out_spec = rows
    consts = (g_mix, g_ffn, g_final, w_in, w_sq, w_sq, w_sq, w_ffn, w_ffn, w_d, ffn_conv_w, halo)
    const_specs = ([_resident(g_mix.shape)] * 3 + [_resident(*W_IN_BLOCKS["gates"])] + square + halves
                   + [_resident(w_d.shape), _resident(ffn_conv_w.shape), _resident(halo.shape)])
    return pl.pallas_call(
        functools.partial(_mixer_out_kernel, tail_at=tail_at),
        grid=(b, n // tile),
        in_specs=[rows, rows, rows] + const_specs,
        out_specs=out_spec,
        out_shape=out_shape,
        scratch_shapes=[pltpu.VMEM((tile + HALO, D_FF), F32)],
        compiler_params=pltpu.CompilerParams(dimension_semantics=("arbitrary", "arbitrary"),
                                             vmem_limit_bytes=VMEM_LIMIT),
        name="mixer_out" if tail_at is None else "mixer_out_meta",
    )(x, att, yc, *consts)


def kernel(x, meta_tokens, g_mix, w_in, b_f, conv_w, w_o_attn, w_o_conv, w_o, g_ffn, w_ffn_in, ffn_conv_w,
           w_ffn_out, g_final):
    assert w_in.shape[0] == 1, "one layer"
    d = D_MODEL
    w_in_packed, w_f = _repack_w_in(w_in[0].T)
    b_f_row = jnp.pad(b_f[0], (0, LANES - N_HEADS))[None, :]
    w_sq = jnp.stack([w_o_conv[0], w_o_attn[0], w_o[0]]).astype(BF16)
    w_ffn = w_ffn_in[0].astype(BF16)
    w_d = w_ffn_out[0].astype(BF16)
    g_mix_row, g_ffn_row, g_final_row = g_mix[0][None, :], g_ffn[0][None, :], g_final[None, :]
    in_consts = (g_mix_row, w_in_packed, w_f, b_f_row, conv_w[0])
    out_consts = (g_mix_row, g_ffn_row, g_final_row, w_in_packed, w_sq, w_ffn, w_d, ffn_conv_w[0])

    meta = jnp.pad(meta_tokens.astype(F32), ((0, META_TILE - N_META), (0, 0)))[None]
    qftm, kfm, vtm, ycm, cm, cu_tail = _mixer_in(
        meta, *in_consts, jnp.zeros((HALO, d), F32), jnp.zeros((1, LANES), F32), tile=META_TILE, tail_at=N_META)
    att_m = _attention(qftm, kfm, vtm, None, tq=META_TILE, heads=HEADS_PER_STEP)
    a_tail = _mixer_out(meta, att_m, ycm, *out_consts, jnp.zeros((HALO, D_FF), F32),
                        tile=META_TILE, tail_at=N_META)

    qft, kf, vt, yc = _mixer_in(x, *in_consts, cu_tail, cm[0, N_META - 1:N_META, :], tile=ROW_TILE_IN)
    att = _attention(qft, kf, vt, (kfm[0], vtm[0]), tq=Q_TILE, heads=HEADS_PER_STEP)
    return _mixer_out(x, att, yc, *out_consts, a_tail, tile=ROW_TILE_OUT)
```
